```python
import math
import jax, jax.numpy as jnp
from jax import lax
import numpy as np

D_MODEL = 1024
BATCH = 8
SEQ = 4096
DEPTH = 2

CHUNK = 64
N_EVEN = (DEPTH + 1) // 2
N_ODD = DEPTH // 2
N_SUB = 3
D_FF = ((8 * D_MODEL // 3 + 127) // 128) * 128
RMS_EPS = 1e-6

GM_BLOCK = 128
GM_WIDTH = D_MODEL
GM_GROUPS = 8
GM_GROUP_DIM = GM_WIDTH // GM_GROUPS

SSD_WIDTH = D_MODEL
SSD_HEAD_DIM = 64
SSD_HEADS = SSD_WIDTH // SSD_HEAD_DIM
SSD_GROUPS = 2
SSD_STATE = 128
SSD_CONV = 4
SSD_CHUNK = CHUNK
SSD_CONV_DIM = SSD_WIDTH + 2 * SSD_GROUPS * SSD_STATE

IN_WIDTH = 2 * GM_WIDTH + SSD_WIDTH + SSD_CONV_DIM + SSD_HEADS
MIX_WIDTH = GM_WIDTH + SSD_WIDTH

SB_WIDTH = D_MODEL
SB_HEAD_DIM = 64
SB_HEADS = SB_WIDTH // SB_HEAD_DIM
SB_BLOCK = 128

kernel_name = "hybrid_gmlp_ssd_stickbreaking_macaron_adaln"


def rms_norm(x, g):
    xf = x.astype(jnp.float32)
    y = xf * lax.rsqrt(jnp.mean(xf * xf, axis=-1, keepdims=True) + RMS_EPS)
    return (y * g.astype(jnp.float32)).astype(x.dtype)


def modulate(h, shift, scale):
    return h * (1 + scale[:, None, :]) + shift[:, None, :]


def swiglu(h, w_gate, w_up, w_down):
    return (jax.nn.silu(h @ w_gate) * (h @ w_up)) @ w_down


def chunked_gmlp(u, v, v_norm_g, w_s, b_s):
    b, s, _ = u.shape
    nc = s // GM_BLOCK
    v = v.reshape(b, nc, GM_BLOCK, GM_GROUPS, GM_GROUP_DIM)
    v = rms_norm(v, v_norm_g.reshape(GM_GROUPS, GM_GROUP_DIM))
    pos = np.arange(GM_BLOCK)
    mask = (pos[None, :] // CHUNK) <= (pos[:, None] // CHUNK)
    w = w_s * jnp.asarray(mask, dtype=w_s.dtype)[None]
    mixed = jnp.einsum('gts,bnsgc->bntgc', w, v) + b_s.T[:, :, None]
    return u * mixed.reshape(b, s, GM_WIDTH)


def causal_dwconv(x, w, bias):
    k = w.shape[0]
    s = x.shape[1]
    xp = jnp.pad(x, ((0, 0), (k - 1, 0), (0, 0)))
    return sum(xp[:, i:i + s] * w[i] for i in range(k)) + bias


def ssd_mixer(z, xbc, dt_raw, conv_w, conv_b, dt_bias, a_log, d_skip, norm_g):
    b, s, _ = z.shape
    nc = s // SSD_CHUNK
    e = SSD_HEADS // SSD_GROUPS
    xbc = jax.nn.silu(causal_dwconv(xbc, conv_w, conv_b))
    xs, bm, cm = jnp.split(xbc, [SSD_WIDTH, SSD_WIDTH + SSD_GROUPS * SSD_STATE], axis=-1)
    xs = xs.reshape(b, nc, SSD_CHUNK, SSD_GROUPS, e, SSD_HEAD_DIM)
    bm = bm.reshape(b, nc, SSD_CHUNK, SSD_GROUPS, SSD_STATE)
    cm = cm.reshape(b, nc, SSD_CHUNK, SSD_GROUPS, SSD_STATE)
    dt = jax.nn.softplus((dt_raw + dt_bias).astype(jnp.float32))
    dt = dt.reshape(b, nc, SSD_CHUNK, SSD_GROUPS, e)
    a = -jnp.exp(a_log.astype(jnp.float32)).reshape(SSD_GROUPS, e)
    cs = jnp.cumsum(dt * a, axis=2)
    seg = cs[:, :, :, None] - cs[:, :, None, :]
    pos = np.arange(SSD_CHUNK)
    causal = (pos[:, None] >= pos[None, :])[None, None, :, :, None, None]
    decay = jnp.exp(jnp.where(causal, seg, -jnp.inf))
    cb = jnp.einsum('bctgn,bcsgn->bctsg', cm, bm)
    w_ts = (cb[..., None] * decay * dt[:, :, None]).astype(xs.dtype)
    y_diag = jnp.einsum('bctsge,bcsgep->bctgep', w_ts, xs)
    decay_to_end = (jnp.exp(cs[:, :, -1:] - cs) * dt).astype(xs.dtype)
    states = jnp.einsum('bclgn,bclge,bclgep->bcgepn', bm, decay_to_end, xs)
    chunk_decay = jnp.exp(cs[:, :, -1]).astype(xs.dtype)

    def step(h, inp):
        st, dec = inp
        return h * dec[..., None, None] + st, h

    h0 = jnp.zeros_like(states[:, 0])
    _, h_in = lax.scan(step, h0, (jnp.moveaxis(states, 1, 0), jnp.moveaxis(chunk_decay, 1, 0)))
    h_in = jnp.moveaxis(h_in, 0, 1)
    y_off = jnp.einsum('bctgn,bcgepn->bctgep', cm, h_in) * jnp.exp(cs).astype(xs.dtype)[..., None]
    y = y_diag + y_off + d_skip.reshape(SSD_GROUPS, e)[:, :, None] * xs
    y = y.reshape(b, s, SSD_WIDTH)
    return rms_norm(y * jax.nn.silu(z), norm_g)


def hybrid_mixer(h, w_in, w_out, v_norm_g, w_s, b_s, conv_w, conv_b, dt_bias, a_log, d_skip, ssd_norm_g):
    proj = h @ w_in
    o1 = 2 * GM_WIDTH
    o2 = o1 + SSD_WIDTH
    o3 = o2 + SSD_CONV_DIM
    gm_part, z, xbc, dt_raw = jnp.split(proj, [o1, o2, o3], axis=-1)
    u, v = jnp.split(jax.nn.gelu(gm_part), 2, axis=-1)
    y_a = chunked_gmlp(u, v, v_norm_g, w_s, b_s)
    y_b = ssd_mixer(z, xbc, dt_raw, conv_w, conv_b, dt_bias, a_log, d_skip, ssd_norm_g)
    return jnp.concatenate([y_a, y_b], axis=-1) @ w_out


def stick_breaking_attention(q, k, v):
    b, nh, s, d = q.shape
    scale = d ** -0.5
    outs = []
    for i in range(s // SB_BLOCK):
        q0 = i * SB_BLOCK
        kv_len = q0 + SB_BLOCK
        qb = q[:, :, q0:kv_len]
        kb = k[:, :, :kv_len]
        vb = v[:, :, :kv_len]
        logits = jnp.einsum('bhtd,bhsd->bhts', qb, kb).astype(jnp.float32) * scale
        t_idx = q0 + np.arange(SB_BLOCK)[:, None]
        s_idx = np.arange(kv_len)[None, :]
        strict = jnp.asarray(s_idx < t_idx)
        log_beta = jax.nn.log_sigmoid(logits)
        log_keep = jnp.where(strict, jax.nn.log_sigmoid(-logits), 0.0)
        after = lax.cumsum(log_keep, axis=3, reverse=True) - log_keep
        wgt = jnp.where(strict, jnp.exp(log_beta + after), 0.0)
        outs.append(jnp.einsum('bhts,bhsd->bhtd', wgt.astype(vb.dtype), vb))
    return jnp.concatenate(outs, axis=2)


def sb_mixer(h, w_qkv, q_norm_g, k_norm_g, w_o):
    b, s, _ = h.shape
    qkv = (h @ w_qkv).reshape(b, s, 3, SB_HEADS, SB_HEAD_DIM)
    q = rms_norm(qkv[:, :, 0], q_norm_g).transpose(0, 2, 1, 3)
    k = rms_norm(qkv[:, :, 1], k_norm_g).transpose(0, 2, 1, 3)
    v = qkv[:, :, 2].transpose(0, 2, 1, 3)
    o = stick_breaking_attention(q, k, v)
    return o.transpose(0, 2, 1, 3).reshape(b, s, SB_WIDTH) @ w_o


def setup_inputs(seed: int = 0) -> dict:
    key = jax.random.key(seed)
    ks = iter(jax.random.split(key, 32))

    def nrm(shape, scale):
        return jax.random.normal(next(ks), shape, jnp.float32) * scale

    d = D_MODEL
    dt0 = jnp.exp(jax.random.uniform(next(ks), (N_EVEN, SSD_HEADS), jnp.float32,
                                     minval=math.log(1e-3), maxval=math.log(1e-1)))
    return {
        "x": nrm((BATCH, SEQ, d), 1.0),
        "c": nrm((BATCH, d), 1.0),
        "mod_w": nrm((DEPTH, d, N_SUB * 3 * d), 0.5 * d ** -0.5),
        "mod_b": nrm((DEPTH, N_SUB * 3 * d), 0.01),
        "norm_g": 1.0 + nrm((DEPTH, 4, d), 0.02),
        "ffn_w_gate": nrm((DEPTH, 2, d, D_FF), d ** -0.5),
        "ffn_w_up": nrm((DEPTH, 2, d, D_FF), d ** -0.5),
        "ffn_w_down": nrm((DEPTH, 2, D_FF, d), D_FF ** -0.5),
        "hy_w_in": nrm((N_EVEN, d, IN_WIDTH), d ** -0.5),
        "hy_w_out": nrm((N_EVEN, MIX_WIDTH, d), MIX_WIDTH ** -0.5),
        "gm_v_norm_g": 1.0 + nrm((N_EVEN, GM_WIDTH), 0.02),
        "gm_w_s": nrm((N_EVEN, GM_GROUPS, GM_BLOCK, GM_BLOCK), GM_BLOCK ** -0.5),
        "gm_b_s": 1.0 + nrm((N_EVEN, GM_GROUPS, GM_BLOCK), 0.02),
        "ssd_conv_w": nrm((N_EVEN, SSD_CONV, SSD_CONV_DIM), SSD_CONV ** -0.5),
        "ssd_conv_b": nrm((N_EVEN, SSD_CONV_DIM), 0.01),
        "ssd_dt_bias": dt0 + jnp.log(-jnp.expm1(-dt0)),
        "ssd_a_log": jnp.log(jax.random.uniform(next(ks), (N_EVEN, SSD_HEADS), jnp.float32,
                                                 minval=1.0, maxval=16.0)),
        "ssd_d": 1.0 + nrm((N_EVEN, SSD_HEADS), 0.1),
        "ssd_norm_g": 1.0 + nrm((N_EVEN, SSD_WIDTH), 0.02),
        "sb_w_qkv": nrm((N_ODD, d, 3 * SB_WIDTH), d ** -0.5),
        "sb_q_norm_g": 1.0 + nrm((N_ODD, SB_HEAD_DIM), 0.02),
        "sb_k_norm_g": 1.0 + nrm((N_ODD, SB_HEAD_DIM), 0.02),
        "sb_w_o": nrm((N_ODD, SB_WIDTH, d), SB_WIDTH ** -0.5),
    }


def reference(x, c, mod_w, mod_b, norm_g, ffn_w_gate, ffn_w_up, ffn_w_down,
              hy_w_in, hy_w_out, gm_v_norm_g, gm_w_s, gm_b_s,
              ssd_conv_w, ssd_conv_b, ssd_dt_bias, ssd_a_log, ssd_d, ssd_norm_g,
              sb_w_qkv, sb_q_norm_g, sb_k_norm_g, sb_w_o):
    cond = jax.nn.silu(c)
    h = x
    for layer in range(DEPTH):
        mod = (cond @ mod_w[layer] + mod_b[layer]).reshape(-1, N_SUB, 3, D_MODEL)
        y = modulate(rms_norm(h, norm_g[layer, 0]), mod[:, 0, 0], mod[:, 0, 1])
        h = h + 0.5 * mod[:, 0, 2][:, None] * swiglu(
            y, ffn_w_gate[layer, 0], ffn_w_up[layer, 0], ffn_w_down[layer, 0])
        y = modulate(rms_norm(h, norm_g[layer, 1]), mod[:, 1, 0], mod[:, 1, 1])
        j = layer // 2
        if layer % 2 == 0:
            mix = hybrid_mixer(y, hy_w_in[j], hy_w_out[j], gm_v_norm_g[j], gm_w_s[j], gm_b_s[j],
                               ssd_conv_w[j], ssd_conv_b[j], ssd_dt_bias[j], ssd_a_log[j],
                               ssd_d[j], ssd_norm_g[j])
        else:
            mix = sb_mixer(y, sb_w_qkv[j], sb_q_norm_g[j], sb_k_norm_g[j], sb_w_o[j])
        h = h + mod[:, 1, 2][:, None] * mix
        y = modulate(rms_norm(h, norm_g[layer, 2]), mod[:, 2, 0], mod[:, 2, 1])
        h = h + 0.5 * mod[:, 2, 2][:, None] * swiglu(
            y, ffn_w_gate[layer, 1], ffn_w_up[layer, 1], ffn_w_down[layer, 1])
        h = rms_norm(h, norm_g[layer, 3])
    return h
```

```python
import functools
import math

import numpy as np
import jax
import jax.numpy as jnp
from jax import lax
from jax.experimental import pallas as pl
from jax.experimental.pallas import tpu as pltpu

F32 = jnp.float32
BF16 = jnp.bfloat16
HIGHEST = lax.Precision.HIGHEST

LANES = 128
VMEM_LIMIT = 56 * 1024 * 1024

RMS_EPS = 1e-6
N_SUB = 3
GM_BLOCK = 128
GM_GROUPS = 8
CHUNK = 64
SSD_HEAD_DIM = 64
SSD_HEADS = 16
SSD_GROUPS = 2
SSD_STATE = 128
SSD_CONV = 4
SSD_L = 128
SB_HEAD_DIM = 64
CONV_HALO = 8

FFN_TM = 512
FFN_FC = 256
HYB_TM = 256
QKV_TM = 512
PROJ_TM = 512
ATT_T = 256


def _silu(x):
    return x / (1.0 + jnp.exp(-x))


def _softplus(x):
    return jnp.maximum(x, 0.0) + jnp.log1p(jnp.exp(-jnp.abs(x)))


def _gelu_tanh(x):
    c = math.sqrt(2.0 / math.pi)
    return x * (0.5 * (1.0 + jnp.tanh(c * (x + 0.044715 * (x * x * x)))))


def _rms(x, g):
    ms = jnp.mean(x * x, axis=-1, keepdims=True)
    return x * lax.rsqrt(ms + RMS_EPS) * g


def _norm_mod(x, g, shift, scale):
    return _rms(x, g) * (1.0 + scale) + shift


def _const_spec(shape):
    nd = len(shape)
    return pl.BlockSpec(shape, lambda *_: (0,) * nd, pipeline_mode=pl.Buffered(1))


def _params(*sem):
    return pltpu.CompilerParams(dimension_semantics=sem, vmem_limit_bytes=VMEM_LIMIT)


def _mod_kernel(c_ref, w_ref, b_ref, o_ref):
    cond = _silu(c_ref[...])
    o_ref[0] = jnp.dot(cond, w_ref[0], precision=HIGHEST, preferred_element_type=F32) + b_ref[0]


def _modulation(c, mod_w, mod_b):
    depth, d, width = mod_w.shape
    b = c.shape[0]
    tn = 1152
    return pl.pallas_call(
        _mod_kernel,
        grid=(depth, width // tn),
        in_specs=[
            pl.BlockSpec((b, d), lambda l, j: (0, 0)),
            pl.BlockSpec((1, d, tn), lambda l, j: (l, 0, j)),
            pl.BlockSpec((1, 1, tn), lambda l, j: (l, 0, j)),
        ],
        out_specs=pl.BlockSpec((1, b, tn), lambda l, j: (l, 0, j)),
        out_shape=jax.ShapeDtypeStruct((depth, b, width), F32),
        compiler_params=_params("arbitrary", "arbitrary"),
        name="modulation",
    )(c, mod_w, mod_b.reshape(depth, 1, width))


def _ffn_kernel(h_ref, mod_ref, g_ref, wg_ref, wu_ref, wd_ref, fg_ref, o_ref, *, final_norm):
    x = h_ref[0]
    y = _norm_mod(x, g_ref[...], mod_ref[0, 0:1, :], mod_ref[0, 1:2, :]).astype(BF16)
    acc = jnp.zeros(x.shape, F32)
    for f0 in range(0, wg_ref.shape[1], FFN_FC):
        a = jnp.dot(y, wg_ref[:, f0:f0 + FFN_FC], preferred_element_type=F32)
        b = jnp.dot(y, wu_ref[:, f0:f0 + FFN_FC], preferred_element_type=F32)
        hid = (_silu(a) * b).astype(BF16)
        acc = acc + jnp.dot(hid, wd_ref[f0:f0 + FFN_FC, :], preferred_element_type=F32)
    out = x + 0.5 * mod_ref[0, 2:3, :] * acc
    if final_norm:
        out = _rms(out, fg_ref[...])
    o_ref[0] = out


def _ffn(h, mod3, g, wg, wu, wd, final_g):
    b, s, d = h.shape
    f = wg.shape[1]
    tm = min(FFN_TM, s)
    final_norm = final_g is not None
    fg = final_g if final_norm else g
    tok = pl.BlockSpec((1, tm, d), lambda bi, i: (bi, i, 0))
    return pl.pallas_call(
        functools.partial(_ffn_kernel, final_norm=final_norm),
        grid=(b, s // tm),
        in_specs=[
            tok,
            pl.BlockSpec((1, N_SUB, d), lambda bi, i: (bi, 0, 0)),
            _const_spec((1, d)),
            _const_spec((d, f)),
            _const_spec((d, f)),
            _const_spec((f, d)),
            _const_spec((1, d)),
        ],
        out_specs=tok,
        out_shape=jax.ShapeDtypeStruct(h.shape, F32),
        compiler_params=_params("parallel", "arbitrary"),
        name="ffn",
    )(h, mod3, g.reshape(1, d), wg, wu, wd, fg.reshape(1, d))


def _hyb_kernel(h_ref, mod_ref, g_ref, wuv_ref, wz_ref, wxbc_ref, wdt_ref, wout_ref,
                vng_ref, ws_ref, bs_ref, convw_ref, convb_ref, dtb_ref, alog_ref, dexp_ref,
                sng_ref, emat_ref, tril_ref, o_ref, conv_buf, state_ref):
    tm = h_ref.shape[1]
    gw = wuv_ref.shape[1] // 2
    sw = wz_ref.shape[1]
    gs = SSD_GROUPS * SSD_STATE
    hpg = SSD_HEADS // SSD_GROUPS
    gcols = hpg * SSD_HEAD_DIM

    @pl.when(pl.program_id(1) == 0)
    def _():
        conv_buf[0:CONV_HALO, :] = jnp.zeros((CONV_HALO, conv_buf.shape[1]), F32)
        state_ref[...] = jnp.zeros(state_ref.shape, F32)

    x = h_ref[0]
    y = _norm_mod(x, g_ref[...], mod_ref[0, 0:1, :], mod_ref[0, 1:2, :]).astype(BF16)

    gm = _gelu_tanh(jnp.dot(y, wuv_ref[...], preferred_element_type=F32))
    u = gm[:, :gw]
    v = gm[:, gw:]
    pos_t = lax.broadcasted_iota(jnp.int32, (GM_BLOCK, GM_BLOCK), 0)
    pos_s = lax.broadcasted_iota(jnp.int32, (GM_BLOCK, GM_BLOCK), 1)
    blk_causal = (pos_s // CHUNK) <= (pos_t // CHUNK)
    ya_cols = []
    for g in range(GM_GROUPS):
        c0 = g * LANES
        vn = _rms(v[:, c0:c0 + LANES], vng_ref[:, c0:c0 + LANES]).astype(BF16)
        wm = jnp.where(blk_causal, ws_ref[g], 0.0).astype(BF16)
        rows = []
        for r0 in range(0, tm, GM_BLOCK):
            mixed = jnp.dot(wm, vn[r0:r0 + GM_BLOCK, :], preferred_element_type=F32)
            rows.append(mixed + bs_ref[:, c0:c0 + LANES])
        ya_cols.append(u[:, c0:c0 + LANES] * jnp.concatenate(rows, axis=0))
    y_a = jnp.concatenate(ya_cols, axis=1).astype(BF16)

    z = jnp.dot(y, wz_ref[...], preferred_element_type=F32)
    conv_buf[CONV_HALO:CONV_HALO + tm, :] = jnp.dot(y, wxbc_ref[...], preferred_element_type=F32)
    dt = _softplus(jnp.dot(y, wdt_ref[...], preferred_element_type=F32) + dtb_ref[...])
    acc = convb_ref[...]
    for k in range(SSD_CONV):
        start = CONV_HALO - (SSD_CONV - 1) + k
        acc = acc + conv_buf[start:start + tm, :] * convw_ref[k:k + 1, :]
    conv_buf[0:CONV_HALO, :] = conv_buf[tm:tm + CONV_HALO, :]
    xbc = _silu(acc)
    xs = xbc[:, :sw]
    bm = xbc[:, sw:sw + gs]
    cm = xbc[:, sw + gs:sw + 2 * gs]
    dta = dt * (-jnp.exp(alog_ref[...]))

    row_i = lax.broadcasted_iota(jnp.int32, (SSD_L, SSD_L), 0)
    col_i = lax.broadcasted_iota(jnp.int32, (SSD_L, SSD_L), 1)
    causal = row_i >= col_i
    lane = lax.broadcasted_iota(jnp.int32, (SSD_L, LANES), 1)
    first_head = lane < SSD_HEAD_DIM
    emat = emat_ref[...]
    yb_rows = []
    for r0 in range(0, tm, SSD_L):
        sl = slice(r0, r0 + SSD_L)
        dt_c = dt[sl]
        cs = jnp.dot(tril_ref[...], dta[sl], precision=HIGHEST, preferred_element_type=F32)
        cs_end = cs[SSD_L - 1:SSD_L, :]
        cs_t = cs.T
        dt_t = dt_c.T
        grow = jnp.dot(jnp.exp(cs), emat, precision=HIGHEST, preferred_element_type=F32)
        gend = jnp.dot(jnp.exp(cs_end - cs) * dt_c, emat, precision=HIGHEST,
                       preferred_element_type=F32)
        xs_c = xs[sl]
        xs_b = xs_c.astype(BF16)
        xd = (xs_c * gend).astype(BF16)
        ycols = []
        for g in range(SSD_GROUPS):
            bm_g = bm[sl, g * SSD_STATE:(g + 1) * SSD_STATE]
            cm_g = cm[sl, g * SSD_STATE:(g + 1) * SSD_STATE].astype(BF16)
            gc = slice(g * gcols, (g + 1) * gcols)
            cb = lax.dot_general(cm_g, bm_g.astype(BF16), (((1,), (1,)), ((), ())),
                                 preferred_element_type=F32)
            st = state_ref[g]
            y_off = jnp.dot(cm_g, st.astype(BF16), preferred_element_type=F32) * grow[:, gc]
            state_ref[g] = st * grow[SSD_L - 1:SSD_L, gc] + jnp.dot(
                bm_g.T.astype(BF16), xd[:, gc], preferred_element_type=F32)
            for p in range(hpg // 2):
                ws = []
                for hd in (g * hpg + 2 * p, g * hpg + 2 * p + 1):
                    seg = cs[:, hd:hd + 1] - cs_t[hd:hd + 1, :]
                    decay = jnp.where(causal, jnp.exp(jnp.where(causal, seg, 0.0)), 0.0)
                    ws.append((cb * decay * dt_t[hd:hd + 1, :]).astype(BF16))
                c0 = (g * hpg + 2 * p) * SSD_HEAD_DIM
                xp = xs_b[:, c0:c0 + LANES]
                zero = jnp.zeros_like(xp)
                rhs = jnp.concatenate([jnp.where(first_head, xp, zero),
                                       jnp.where(first_head, zero, xp)], axis=0)
                y_diag = jnp.dot(jnp.concatenate(ws, axis=1), rhs, preferred_element_type=F32)
                ycols.append(y_diag + y_off[:, 2 * p * SSD_HEAD_DIM:2 * p * SSD_HEAD_DIM + LANES])
        y_c = jnp.concatenate(ycols, axis=1) + dexp_ref[...] * xs_c
        yb_rows.append(_rms(y_c * _silu(z[sl]), sng_ref[...]))
    y_b = jnp.concatenate(yb_rows, axis=0).astype(BF16)

    mix = (jnp.dot(y_a, wout_ref[0:gw, :], preferred_element_type=F32)
           + jnp.dot(y_b, wout_ref[gw:gw + sw, :], preferred_element_type=F32))
    o_ref[0] = x + mod_ref[0, 2:3, :] * mix


def _hybrid(h, mod3, g, w_in, w_out, v_norm_g, w_s, b_s, conv_w, conv_b, dt_bias, a_log, d_skip,
            ssd_norm_g):
    b, s, d = h.shape
    gw = v_norm_g.shape[0]
    sw = ssd_norm_g.shape[0]
    cdim = conv_w.shape[1]
    tm = min(HYB_TM, s)
    o1 = 2 * gw
    o2 = o1 + sw
    o3 = o2 + cdim
    pad = LANES - SSD_HEADS
    w_in_b = w_in.astype(BF16)
    w_dt = jnp.pad(w_in_b[:, o3:], ((0, 0), (0, pad)))
    dtb = jnp.pad(dt_bias, (0, pad)).reshape(1, LANES)
    alog = jnp.pad(a_log, (0, pad)).reshape(1, LANES)
    dexp = jnp.repeat(d_skip, SSD_HEAD_DIM).reshape(1, sw)
    bs_full = jnp.repeat(b_s.T, gw // GM_GROUPS, axis=1)
    emat = np.zeros((LANES, sw), np.float32)
    for hd in range(SSD_HEADS):
        emat[hd, hd * SSD_HEAD_DIM:(hd + 1) * SSD_HEAD_DIM] = 1.0
    tril = np.tril(np.ones((SSD_L, SSD_L), np.float32))
    tok = pl.BlockSpec((1, tm, d), lambda bi, i: (bi, i, 0))
    operands = [
        g.reshape(1, d), w_in_b[:, :o1], w_in_b[:, o1:o2], w_in_b[:, o2:o3], w_dt,
        w_out.astype(BF16), v_norm_g.reshape(1, gw), w_s, bs_full, conv_w,
        conv_b.reshape(1, cdim), dtb, alog, dexp, ssd_norm_g.reshape(1, sw),
        jnp.asarray(emat), jnp.asarray(tril),
    ]
    return pl.pallas_call(
        _hyb_kernel,
        grid=(b, s // tm),
        in_specs=[tok, pl.BlockSpec((1, N_SUB, d), lambda bi, i: (bi, 0, 0))]
        + [_const_spec(op.shape) for op in operands],
        out_specs=tok,
        out_shape=jax.ShapeDtypeStruct(h.shape, F32),
        scratch_shapes=[
            pltpu.VMEM((tm + CONV_HALO, cdim), F32),
            pltpu.VMEM((SSD_GROUPS, SSD_STATE, sw // SSD_GROUPS), F32),
        ],
        compiler_params=_params("parallel", "arbitrary"),
        name="hybrid_mixer",
    )(h, mod3, *operands)


def _qkv_kernel(h_ref, mod_ref, g_ref, w_ref, o_ref):
    y = _norm_mod(h_ref[0], g_ref[...], mod_ref[0, 0:1, :], mod_ref[0, 1:2, :]).astype(BF16)
    o_ref[0] = jnp.dot(y, w_ref[...], preferred_element_type=F32)


def _qkv(h, mod3, g, w):
    b, s, d = h.shape
    n = w.shape[1]
    tm = min(QKV_TM, s)
    return pl.pallas_call(
        _qkv_kernel,
        grid=(b, s // tm),
        in_specs=[
            pl.BlockSpec((1, tm, d), lambda bi, i: (bi, i, 0)),
            pl.BlockSpec((1, N_SUB, d), lambda bi, i: (bi, 0, 0)),
            _const_spec((1, d)),
            _const_spec((d, n)),
        ],
        out_specs=pl.BlockSpec((1, tm, n), lambda bi, i: (bi, i, 0)),
        out_shape=jax.ShapeDtypeStruct((b, s, n), F32),
        compiler_params=_params("parallel", "arbitrary"),
        name="qkv_proj",
    )(h, mod3, g.reshape(1, d), w)


def _pair_rms(x, g, first_head):
    xx = x * x
    s0 = jnp.sum(jnp.where(first_head, xx, 0.0), axis=-1, keepdims=True)
    s1 = jnp.sum(jnp.where(first_head, 0.0, xx), axis=-1, keepdims=True)
    ms = jnp.where(first_head, s0, s1) * (1.0 / SB_HEAD_DIM)
    return x * lax.rsqrt(ms + RMS_EPS) * g


def _attn_kernel(q_ref, k_ref, v_ref, qg_ref, kg_ref, u_ref, o_ref, kn_ref, vb_ref, acc_ref,
                 carry_ref):
    t = q_ref.shape[1]
    i = pl.program_id(2)
    first_head = lax.broadcasted_iota(jnp.int32, (1, LANES), 1) < SB_HEAD_DIM

    @pl.when(i == 0)
    def _():
        kn_ref[...] = _pair_rms(k_ref[0], kg_ref[...], first_head).astype(BF16)
        vb_ref[...] = v_ref[0].astype(BF16)

    qn = _pair_rms(q_ref[0], qg_ref[...], first_head) * (SB_HEAD_DIM ** -0.5)
    row = lax.broadcasted_iota(jnp.int32, (t, t), 0)
    col = lax.broadcasted_iota(jnp.int32, (t, t), 1)
    outs = []
    for hd in range(2):
        keep = first_head if hd == 0 else jnp.logical_not(first_head)
        q_h = jnp.where(keep, qn, 0.0).astype(BF16)
        acc_ref[...] = jnp.zeros(acc_ref.shape, F32)
        carry_ref[...] = jnp.zeros(carry_ref.shape, F32)

        def body(jj, _, q_h=q_h):
            j = i - jj
            k0 = pl.multiple_of(j * t, t)
            z = lax.dot_general(q_h, kn_ref[pl.ds(k0, t), :], (((1,), (1,)), ((), ())),
                                preferred_element_type=F32)
            strict = (col + (j - i) * t) < row
            log_beta = jnp.minimum(z, 0.0) - jnp.log1p(jnp.exp(-jnp.abs(z)))
            log_keep = jnp.where(strict, log_beta - z, 0.0)
            hi = log_keep.astype(BF16)
            lo = (log_keep - hi.astype(F32)).astype(BF16)
            after = (jnp.dot(hi, u_ref[...], preferred_element_type=F32)
                     + jnp.dot(lo, u_ref[...], preferred_element_type=F32)
                     + carry_ref[:, 0:1])
            wgt = jnp.where(strict, jnp.exp(log_beta + after), 0.0).astype(BF16)
            acc_ref[...] += jnp.dot(wgt, vb_ref[pl.ds(k0, t), :], preferred_element_type=F32)
            carry_ref[...] += jnp.sum(log_keep, axis=-1, keepdims=True)
            return 0

        lax.fori_loop(0, i + 1, body, 0)
        outs.append(acc_ref[...])
    o_ref[0] = jnp.where(first_head, outs[0], outs[1]).astype(o_ref.dtype)


def _attention(qkv, q_norm_g, k_norm_g):
    b, s, n = qkv.shape
    width = n // 3
    npairs = width // LANES
    t = min(ATT_T, s)
    qg = jnp.tile(q_norm_g, 2).reshape(1, LANES)
    kg = jnp.tile(k_norm_g, 2).reshape(1, LANES)
    umat = np.triu(np.ones((t, t), np.float32), 0).T - np.eye(t, dtype=np.float32)
    return pl.pallas_call(
        _attn_kernel,
        grid=(b, npairs, s // t),
        in_specs=[
            pl.BlockSpec((1, t, LANES), lambda bi, p, i: (bi, i, p)),
            pl.BlockSpec((1, s, LANES), lambda bi, p, i: (bi, 0, npairs + p)),
            pl.BlockSpec((1, s, LANES), lambda bi, p, i: (bi, 0, 2 * npairs + p)),
            _const_spec((1, LANES)),
            _const_spec((1, LANES)),
            _const_spec((t, t)),
        ],
        out_specs=pl.BlockSpec((1, t, LANES), lambda bi, p, i: (bi, i, p)),
        out_shape=jax.ShapeDtypeStruct((b, s, width), BF16),
        scratch_shapes=[
            pltpu.VMEM((s, LANES), BF16),
            pltpu.VMEM((s, LANES), BF16),
            pltpu.VMEM((t, LANES), F32),
            pltpu.VMEM((t, LANES), F32),
        ],
        compiler_params=_params("parallel", "parallel", "arbitrary"),
        name="stickbreaking_attention",
    )(qkv, qkv, qkv, qg, kg, jnp.asarray(umat, BF16))


def _proj_res_kernel(h_ref, a_ref, mod_ref, w_ref, o_ref):
    mix = jnp.dot(a_ref[0], w_ref[...], preferred_element_type=F32)
    o_ref[0] = h_ref[0] + mod_ref[0, 2:3, :] * mix


def _proj_residual(h, a, mod3, w):
    b, s, d = h.shape
    k = a.shape[2]
    tm = min(PROJ_TM, s)
    return pl.pallas_call(
        _proj_res_kernel,
        grid=(b, s // tm),
        in_specs=[
            pl.BlockSpec((1, tm, d), lambda bi, i: (bi, i, 0)),
            pl.BlockSpec((1, tm, k), lambda bi, i: (bi, i, 0)),
            pl.BlockSpec((1, N_SUB, d), lambda bi, i: (bi, 0, 0)),
            _const_spec((k, d)),
        ],
        out_specs=pl.BlockSpec((1, tm, d), lambda bi, i: (bi, i, 0)),
        out_shape=jax.ShapeDtypeStruct(h.shape, F32),
        compiler_params=_params("parallel", "arbitrary"),
        name="attn_out_proj",
    )(h, a, mod3, w)


def kernel(x, c, mod_w, mod_b, norm_g, ffn_w_gate, ffn_w_up, ffn_w_down, hy_w_in, hy_w_out, gm_v_norm_g, gm_w_s, gm_b_s, ssd_conv_w, ssd_conv_b, ssd_dt_bias, ssd_a_log, ssd_d, ssd_norm_g, sb_w_qkv, sb_q_norm_g, sb_k_norm_g, sb_w_o):
    depth = mod_w.shape[0]
    b, s, d = x.shape
    mod = _modulation(c, mod_w, mod_b).reshape(depth, b, N_SUB, 3, d)
    wg = ffn_w_gate.astype(BF16)
    wu = ffn_w_up.astype(BF16)
    wd = ffn_w_down.astype(BF16)
    h = x
    for layer in range(depth):
        j = layer // 2
        h = _ffn(h, mod[layer, :, 0], norm_g[layer, 0], wg[layer, 0], wu[layer, 0], wd[layer, 0], None)
        if layer % 2 == 0:
            h = _hybrid(h, mod[layer, :, 1], norm_g[layer, 1], hy_w_in[j], hy_w_out[j],
                        gm_v_norm_g[j], gm_w_s[j], gm_b_s[j], ssd_conv_w[j], ssd_conv_b[j],
                        ssd_dt_bias[j], ssd_a_log[j], ssd_d[j], ssd_norm_g[j])
        else:
            qkv = _qkv(h, mod[layer, :, 1], norm_g[layer, 1], sb_w_qkv[j].astype(BF16))
            o = _attention(qkv, sb_q_norm_g[j], sb_k_norm_g[j])
            h = _proj_residual(h, o, mod[layer, :, 1], sb_w_o[j].astype(BF16))
        h = _ffn(h, mod[layer, :, 2], norm_g[layer, 2], wg[layer, 1], wu[layer, 1], wd[layer, 1],
                 norm_g[layer, 3])
    return h
```

```python
import functools
import math

import numpy as np
import jax
import jax.numpy as jnp
from jax import lax
from jax.experimental import pallas as pl
from jax.experimental.pallas import tpu as pltpu

F32 = jnp.float32
BF16 = jnp.bfloat16
HIGHEST = lax.Precision.HIGHEST

LANES = 128
VMEM_LIMIT = 56 * 1024 * 1024

RMS_EPS = 1e-6
N_SUB = 3
GM_BLOCK = 128
GM_GROUPS = 8
CHUNK = 64
SSD_HEAD_DIM = 64
SSD_HEADS = 16
SSD_GROUPS = 2
SSD_STATE = 128
SSD_CONV = 4
SSD_L = 128
SB_HEAD_DIM = 64
SB_EXP_FLOOR = -105.0
CONV_HALO = 8

FFN_TM = 512
FFN_FC = 256
HYB_TM = 256
QKV_TM = 512
PROJ_TM = 512
ATT_T = 256


def _silu(x):
    return x / (1.0 + jnp.exp(-x))


def _softplus(x):
    return jnp.maximum(x, 0.0) + jnp.log1p(jnp.exp(-jnp.abs(x)))


def _gelu_tanh(x):
    c = math.sqrt(2.0 / math.pi)
    return x * (0.5 * (1.0 + jnp.tanh(c * (x + 0.044715 * (x * x * x)))))


def _rms(x, g):
    ms = jnp.mean(x * x, axis=-1, keepdims=True)
    return x * lax.rsqrt(ms + RMS_EPS) * g


def _bf16_split(x):
    hi = x.astype(BF16).astype(F32)
    return hi, x - hi


def _norm_mod(x, g, shift, scale):
    return _rms(x, g) * (1.0 + scale) + shift


def _const_spec(shape):
    nd = len(shape)
    return pl.BlockSpec(shape, lambda *_: (0,) * nd, pipeline_mode=pl.Buffered(1))


def _params(*sem):
    return pltpu.CompilerParams(dimension_semantics=sem, vmem_limit_bytes=VMEM_LIMIT)


def _mod_kernel(c_ref, w_ref, b_ref, o_ref):
    cond = _silu(c_ref[...])
    o_ref[0] = jnp.dot(cond, w_ref[0], precision=HIGHEST, preferred_element_type=F32) + b_ref[0]


def _modulation(c, mod_w, mod_b):
    depth, d, width = mod_w.shape
    b = c.shape[0]
    tn = 1152
    return pl.pallas_call(
        _mod_kernel,
        grid=(depth, width // tn),
        in_specs=[
            pl.BlockSpec((b, d), lambda l, j: (0, 0)),
            pl.BlockSpec((1, d, tn), lambda l, j: (l, 0, j)),
            pl.BlockSpec((1, 1, tn), lambda l, j: (l, 0, j)),
        ],
        out_specs=pl.BlockSpec((1, b, tn), lambda l, j: (l, 0, j)),
        out_shape=jax.ShapeDtypeStruct((depth, b, width), F32),
        compiler_params=_params("arbitrary", "arbitrary"),
        name="modulation",
    )(c, mod_w, mod_b.reshape(depth, 1, width))


def _ffn_kernel(h_ref, mod_ref, g_ref, wg_ref, wu_ref, wd_ref, fg_ref, o_ref, *, final_norm):
    x = h_ref[0]
    y = _norm_mod(x, g_ref[...], mod_ref[0, 0:1, :], mod_ref[0, 1:2, :]).astype(BF16)
    acc = jnp.zeros(x.shape, F32)
    for f0 in range(0, wg_ref.shape[1], FFN_FC):
        a = jnp.dot(y, wg_ref[:, f0:f0 + FFN_FC], preferred_element_type=F32)
        b = jnp.dot(y, wu_ref[:, f0:f0 + FFN_FC], preferred_element_type=F32)
        hid = (_silu(a) * b).astype(BF16)
        acc = acc + jnp.dot(hid, wd_ref[f0:f0 + FFN_FC, :], preferred_element_type=F32)
    out = x + 0.5 * mod_ref[0, 2:3, :] * acc
    if final_norm:
        out = _rms(out, fg_ref[...])
    o_ref[0] = out


def _ffn(h, mod3, g, wg, wu, wd, final_g):
    b, s, d = h.shape
    f = wg.shape[1]
    tm = min(FFN_TM, s)
    final_norm = final_g is not None
    fg = final_g if final_norm else g
    tok = pl.BlockSpec((1, tm, d), lambda bi, i: (bi, i, 0))
    return pl.pallas_call(
        functools.partial(_ffn_kernel, final_norm=final_norm),
        grid=(b, s // tm),
        in_specs=[
            tok,
            pl.BlockSpec((1, N_SUB, d), lambda bi, i: (bi, 0, 0)),
            _const_spec((1, d)),
            _const_spec((d, f)),
            _const_spec((d, f)),
            _const_spec((f, d)),
            _const_spec((1, d)),
        ],
        out_specs=tok,
        out_shape=jax.ShapeDtypeStruct(h.shape, F32),
        compiler_params=_params("parallel", "arbitrary"),
        name="ffn",
    )(h, mod3, g.reshape(1, d), wg, wu, wd, fg.reshape(1, d))


def _hyb_kernel(h_ref, mod_ref, g_ref, wuv_ref, wz_ref, wxbc_ref, wdt_ref, wout_ref,
                vng_ref, ws_ref, bs_ref, convw_ref, convb_ref, dtb_ref, alog_ref, dexp_ref,
                sng_ref, emat_ref, tril_ref, o_ref, conv_buf, state_ref):
    tm = h_ref.shape[1]
    gw = wuv_ref.shape[1] // 2
    sw = wz_ref.shape[1]
    gs = SSD_GROUPS * SSD_STATE
    hpg = SSD_HEADS // SSD_GROUPS
    gcols = hpg * SSD_HEAD_DIM

    @pl.when(pl.program_id(1) == 0)
    def _():
        conv_buf[0:CONV_HALO, :] = jnp.zeros((CONV_HALO, conv_buf.shape[1]), F32)
        state_ref[...] = jnp.zeros(state_ref.shape, F32)

    x = h_ref[0]
    y = _norm_mod(x, g_ref[...], mod_ref[0, 0:1, :], mod_ref[0, 1:2, :]).astype(BF16)

    gm = _gelu_tanh(jnp.dot(y, wuv_ref[...], preferred_element_type=F32))
    u = gm[:, :gw]
    v = gm[:, gw:]
    pos_t = lax.broadcasted_iota(jnp.int32, (GM_BLOCK, GM_BLOCK), 0)
    pos_s = lax.broadcasted_iota(jnp.int32, (GM_BLOCK, GM_BLOCK), 1)
    blk_causal = (pos_s // CHUNK) <= (pos_t // CHUNK)
    ya_cols = []
    for g in range(GM_GROUPS):
        c0 = g * LANES
        vn = _rms(v[:, c0:c0 + LANES], vng_ref[:, c0:c0 + LANES]).astype(BF16)
        wm = jnp.where(blk_causal, ws_ref[g], 0.0).astype(BF16)
        rows = []
        for r0 in range(0, tm, GM_BLOCK):
            mixed = jnp.dot(wm, vn[r0:r0 + GM_BLOCK, :], preferred_element_type=F32)
            rows.append(mixed + bs_ref[:, c0:c0 + LANES])
        ya_cols.append(u[:, c0:c0 + LANES] * jnp.concatenate(rows, axis=0))
    y_a = jnp.concatenate(ya_cols, axis=1).astype(BF16)

    z = jnp.dot(y, wz_ref[...], preferred_element_type=F32)
    conv_buf[CONV_HALO:CONV_HALO + tm, :] = jnp.dot(y, wxbc_ref[...], preferred_element_type=F32)
    dt = _softplus(jnp.dot(y, wdt_ref[...], preferred_element_type=F32) + dtb_ref[...])
    acc = convb_ref[...]
    for k in range(SSD_CONV):
        start = CONV_HALO - (SSD_CONV - 1) + k
        acc = acc + conv_buf[start:start + tm, :] * convw_ref[k:k + 1, :]
    conv_buf[0:CONV_HALO, :] = conv_buf[tm:tm + CONV_HALO, :]
    xbc = _silu(acc)
    xs = xbc[:, :sw]
    bm = xbc[:, sw:sw + gs]
    cm = xbc[:, sw + gs:sw + 2 * gs]
    dta = dt * (-jnp.exp(alog_ref[...]))

    row_i = lax.broadcasted_iota(jnp.int32, (SSD_L, SSD_L), 0)
    col_i = lax.broadcasted_iota(jnp.int32, (SSD_L, SSD_L), 1)
    causal = row_i >= col_i
    lane = lax.broadcasted_iota(jnp.int32, (SSD_L, LANES), 1)
    first_head = lane < SSD_HEAD_DIM
    yb_rows = []
    for r0 in range(0, tm, SSD_L):
        sl = slice(r0, r0 + SSD_L)
        dt_c = dt[sl]
        t0, rest = _bf16_split(dta[sl])
        t1, rest = _bf16_split(rest)
        terms = jnp.concatenate([t0, t1, rest], axis=1).astype(BF16)
        sums = jnp.dot(tril_ref[...], terms, preferred_element_type=F32)
        cs = sums[:, :LANES] + sums[:, LANES:2 * LANES] + sums[:, 2 * LANES:]
        cs_end = cs[SSD_L - 1:SSD_L, :]
        cs_t = cs.T
        dt_t = dt_c.T
        p1h, p1l = _bf16_split(jnp.exp(cs))
        p2h, p2l = _bf16_split(jnp.exp(cs_end - cs) * dt_c)
        packed = jnp.where(
            lane < SSD_HEADS, p1h,
            jnp.where(lane < 2 * SSD_HEADS, pltpu.roll(p1l, SSD_HEADS, 1),
                      jnp.where(lane < 3 * SSD_HEADS, pltpu.roll(p2h, 2 * SSD_HEADS, 1),
                                pltpu.roll(p2l, 3 * SSD_HEADS, 1))))
        spread = jnp.dot(packed.astype(BF16), emat_ref[...], preferred_element_type=F32)
        grow = spread[:, :sw]
        gend = spread[:, sw:]
        xs_c = xs[sl]
        xs_b = xs_c.astype(BF16)
        xd = (xs_c * gend).astype(BF16)
        ycols = []
        for g in range(SSD_GROUPS):
            bm_g = bm[sl, g * SSD_STATE:(g + 1) * SSD_STATE]
            cm_g = cm[sl, g * SSD_STATE:(g + 1) * SSD_STATE].astype(BF16)
            gc = slice(g * gcols, (g + 1) * gcols)
            cb = lax.dot_general(cm_g, bm_g.astype(BF16), (((1,), (1,)), ((), ())),
                                 preferred_element_type=F32)
            st = state_ref[g]
            y_off = jnp.dot(cm_g, st.astype(BF16), preferred_element_type=F32) * grow[:, gc]
            state_ref[g] = st * grow[SSD_L - 1:SSD_L, gc] + jnp.dot(
                bm_g.T.astype(BF16), xd[:, gc], preferred_element_type=F32)
            for p in range(hpg // 2):
                ws = []
                for hd in (g * hpg + 2 * p, g * hpg + 2 * p + 1):
                    seg = cs[:, hd:hd + 1] - cs_t[hd:hd + 1, :]
                    decay = jnp.where(causal, jnp.exp(jnp.where(causal, seg, 0.0)), 0.0)
                    ws.append((cb * decay * dt_t[hd:hd + 1, :]).astype(BF16))
                c0 = (g * hpg + 2 * p) * SSD_HEAD_DIM
                xp = xs_b[:, c0:c0 + LANES]
                zero = jnp.zeros_like(xp)
                rhs = jnp.concatenate([jnp.where(first_head, xp, zero),
                                       jnp.where(first_head, zero, xp)], axis=0)
                y_diag = jnp.dot(jnp.concatenate(ws, axis=1), rhs, preferred_element_type=F32)
                ycols.append(y_diag + y_off[:, 2 * p * SSD_HEAD_DIM:2 * p * SSD_HEAD_DIM + LANES])
        y_c = jnp.concatenate(ycols, axis=1) + dexp_ref[...] * xs_c
        yb_rows.append(_rms(y_c * _silu(z[sl]), sng_ref[...]))
    y_b = jnp.concatenate(yb_rows, axis=0).astype(BF16)

    mix = (jnp.dot(y_a, wout_ref[0:gw, :], preferred_element_type=F32)
           + jnp.dot(y_b, wout_ref[gw:gw + sw, :], preferred_element_type=F32))
    o_ref[0] = x + mod_ref[0, 2:3, :] * mix


def _hybrid(h, mod3, g, w_in, w_out, v_norm_g, w_s, b_s, conv_w, conv_b, dt_bias, a_log, d_skip,
            ssd_norm_g):
    b, s, d = h.shape
    gw = v_norm_g.shape[0]
    sw = ssd_norm_g.shape[0]
    cdim = conv_w.shape[1]
    tm = min(HYB_TM, s)
    o1 = 2 * gw
    o2 = o1 + sw
    o3 = o2 + cdim
    pad = LANES - SSD_HEADS
    w_in_b = w_in.astype(BF16)
    w_dt = jnp.pad(w_in_b[:, o3:], ((0, 0), (0, pad)))
    dtb = jnp.pad(dt_bias, (0, pad)).reshape(1, LANES)
    alog = jnp.pad(a_log, (0, pad)).reshape(1, LANES)
    dexp = jnp.repeat(d_skip, SSD_HEAD_DIM).reshape(1, sw)
    bs_full = jnp.repeat(b_s.T, gw // GM_GROUPS, axis=1)
    emat = np.zeros((LANES, 2 * sw), np.float32)
    for part in range(4):
        for hd in range(SSD_HEADS):
            c0 = (part // 2) * sw + hd * SSD_HEAD_DIM
            emat[part * SSD_HEADS + hd, c0:c0 + SSD_HEAD_DIM] = 1.0
    tril = np.tril(np.ones((SSD_L, SSD_L), np.float32))
    tok = pl.BlockSpec((1, tm, d), lambda bi, i: (bi, i, 0))
    operands = [
        g.reshape(1, d), w_in_b[:, :o1], w_in_b[:, o1:o2], w_in_b[:, o2:o3], w_dt,
        w_out.astype(BF16), v_norm_g.reshape(1, gw), w_s, bs_full, conv_w,
        conv_b.reshape(1, cdim), dtb, alog, dexp, ssd_norm_g.reshape(1, sw),
        jnp.asarray(emat, BF16), jnp.asarray(tril, BF16),
    ]
    return pl.pallas_call(
        _hyb_kernel,
        grid=(b, s // tm),
        in_specs=[tok, pl.BlockSpec((1, N_SUB, d), lambda bi, i: (bi, 0, 0))]
        + [_const_spec(op.shape) for op in operands],
        out_specs=tok,
        out_shape=jax.ShapeDtypeStruct(h.shape, F32),
        scratch_shapes=[
            pltpu.VMEM((tm + CONV_HALO, cdim), F32),
            pltpu.VMEM((SSD_GROUPS, SSD_STATE, sw // SSD_GROUPS), F32),
        ],
        compiler_params=_params("parallel", "arbitrary"),
        name="hybrid_mixer",
    )(h, mod3, *operands)


def _pair_rms(x, g, first_head):
    xx = x * x
    s0 = jnp.sum(jnp.where(first_head, xx, 0.0), axis=-1, keepdims=True)
    s1 = jnp.sum(jnp.where(first_head, 0.0, xx), axis=-1, keepdims=True)
    ms = jnp.where(first_head, s0, s1) * (1.0 / SB_HEAD_DIM)
    return x * lax.rsqrt(ms + RMS_EPS) * g


def _qkv_kernel(h_ref, mod_ref, g_ref, w_ref, qg_ref, kg_ref, q_ref, k_ref, v_ref):
    y = _norm_mod(h_ref[0], g_ref[...], mod_ref[0, 0:1, :], mod_ref[0, 1:2, :]).astype(BF16)
    width = q_ref.shape[2]
    first_head = lax.broadcasted_iota(jnp.int32, (1, LANES), 1) < SB_HEAD_DIM
    for part, (out, gain) in enumerate(((q_ref, qg_ref[...] * (SB_HEAD_DIM ** -0.5)),
                                        (k_ref, kg_ref[...]))):
        a = jnp.dot(y, w_ref[:, part * width:(part + 1) * width], preferred_element_type=F32)
        for c0 in range(0, width, LANES):
            out[0, :, c0:c0 + LANES] = _pair_rms(a[:, c0:c0 + LANES], gain, first_head).astype(BF16)
    v_ref[0] = jnp.dot(y, w_ref[:, 2 * width:3 * width], preferred_element_type=F32).astype(BF16)


def _qkv(h, mod3, g, w, q_norm_g, k_norm_g):
    b, s, d = h.shape
    width = w.shape[1] // 3
    tm = min(QKV_TM, s)
    out = pl.BlockSpec((1, tm, width), lambda bi, i: (bi, i, 0))
    return pl.pallas_call(
        _qkv_kernel,
        grid=(b, s // tm),
        in_specs=[
            pl.BlockSpec((1, tm, d), lambda bi, i: (bi, i, 0)),
            pl.BlockSpec((1, N_SUB, d), lambda bi, i: (bi, 0, 0)),
            _const_spec((1, d)),
            _const_spec((d, 3 * width)),
            _const_spec((1, LANES)),
            _const_spec((1, LANES)),
        ],
        out_specs=[out, out, out],
        out_shape=[jax.ShapeDtypeStruct((b, s, width), BF16)] * 3,
        compiler_params=_params("parallel", "arbitrary"),
        name="qkv_proj",
    )(h, mod3, g.reshape(1, d), w, jnp.tile(q_norm_g, 2).reshape(1, LANES),
      jnp.tile(k_norm_g, 2).reshape(1, LANES))


def _sb_tile(q_h, k_t, v_t, u, carry, mask):
    z = lax.dot_general(q_h, k_t, (((1,), (1,)), ((), ())), preferred_element_type=F32)
    log_beta = jnp.minimum(z, 0.0) - jnp.log(1.0 + jnp.exp(-jnp.abs(z)))
    log_keep = log_beta - z
    if mask is not None:
        log_keep = jnp.where(mask, log_keep, 0.0)
    hi = log_keep.astype(BF16)
    lo = (log_keep - hi.astype(F32)).astype(BF16)
    after = jnp.dot(jnp.concatenate([hi, lo], axis=1), u, preferred_element_type=F32)
    if carry is not None:
        after = after + carry
    wgt = jnp.exp(log_beta + after)
    if mask is not None:
        wgt = jnp.where(mask, wgt, 0.0)
    pv = jnp.dot(wgt.astype(BF16), v_t, preferred_element_type=F32)
    return pv, jnp.sum(log_keep, axis=-1, keepdims=True)


def _attn_kernel(q_ref, k_ref, v_ref, u_ref, o_ref, acc_ref, carry_ref):
    t = q_ref.shape[1]
    i = pl.program_id(2)
    first_head = lax.broadcasted_iota(jnp.int32, (1, LANES), 1) < SB_HEAD_DIM
    q = q_ref[0]
    zero = jnp.zeros_like(q)
    q_heads = (jnp.where(first_head, q, zero), jnp.where(first_head, zero, q))
    u = u_ref[...]
    diag = (lax.broadcasted_iota(jnp.int32, (t, t), 1) < lax.broadcasted_iota(jnp.int32, (t, t), 0))

    def kv_tile(j):
        k0 = pl.multiple_of(j * t, t)
        return k_ref[0, pl.ds(k0, t), :], v_ref[0, pl.ds(k0, t), :]

    def store_out():
        o_ref[0] = jnp.where(first_head, acc_ref[0], acc_ref[1]).astype(o_ref.dtype)

    @pl.when(i == 0)
    def _():
        k_d, v_d = kv_tile(i)
        for hd in range(2):
            acc_ref[hd] = _sb_tile(q_heads[hd], k_d, v_d, u, None, diag)[0]
        store_out()

    @pl.when(i > 0)
    def _():
        k_d, v_d = kv_tile(i)
        k_p, v_p = kv_tile(i - 1)
        worst = None
        for hd in range(2):
            pv_d, c_d = _sb_tile(q_heads[hd], k_d, v_d, u, None, diag)
            pv_p, c_p = _sb_tile(q_heads[hd], k_p, v_p, u, c_d, None)
            acc_ref[hd] = pv_d + pv_p
            c = c_d + c_p
            carry_ref[hd] = jnp.broadcast_to(c, (t, LANES))
            worst = c if worst is None else jnp.maximum(worst, c)

        def cond(state):
            j, top = state
            return jnp.logical_and(j >= 0, top > SB_EXP_FLOOR)

        def body(state):
            j, _ = state
            k_j, v_j = kv_tile(j)
            worst = None
            for hd in range(2):
                c = carry_ref[hd][:, 0:1]
                pv, rs = _sb_tile(q_heads[hd], k_j, v_j, u, c, None)
                acc_ref[hd] += pv
                c = c + rs
                carry_ref[hd] = jnp.broadcast_to(c, (t, LANES))
                worst = c if worst is None else jnp.maximum(worst, c)
            return j - 1, jnp.max(worst)

        lax.while_loop(cond, body, (i - 2, jnp.max(worst)))
        store_out()


def _attention(q, k, v):
    b, s, width = q.shape
    npairs = width // LANES
    t = min(ATT_T, s)
    umat = np.tril(np.ones((t, t), np.float32), -1)
    umat = np.concatenate([umat, umat], axis=0)
    q_spec = pl.BlockSpec((1, t, LANES), lambda bi, p, i: (bi, i, p))
    kv_spec = pl.BlockSpec((1, s, LANES), lambda bi, p, i: (bi, 0, p))
    return pl.pallas_call(
        _attn_kernel,
        grid=(b, npairs, s // t),
        in_specs=[q_spec, kv_spec, kv_spec, _const_spec((2 * t, t))],
        out_specs=q_spec,
        out_shape=jax.ShapeDtypeStruct((b, s, width), BF16),
        scratch_shapes=[
            pltpu.VMEM((2, t, LANES), F32),
            pltpu.VMEM((2, t, LANES), F32),
        ],
        compiler_params=_params("parallel", "parallel", "arbitrary"),
        name="stickbreaking_attention",
    )(q, k, v, jnp.asarray(umat, BF16))


def _proj_res_kernel(h_ref, a_ref, mod_ref, w_ref, o_ref):
    mix = jnp.dot(a_ref[0], w_ref[...], preferred_element_type=F32)
    o_ref[0] = h_ref[0] + mod_ref[0, 2:3, :] * mix


def _proj_residual(h, a, mod3, w):
    b, s, d = h.shape
    k = a.shape[2]
    tm = min(PROJ_TM, s)
    return pl.pallas_call(
        _proj_res_kernel,
        grid=(b, s // tm),
        in_specs=[
            pl.BlockSpec((1, tm, d), lambda bi, i: (bi, i, 0)),
            pl.BlockSpec((1, tm, k), lambda bi, i: (bi, i, 0)),
            pl.BlockSpec((1, N_SUB, d), lambda bi, i: (bi, 0, 0)),
            _const_spec((k, d)),
        ],
        out_specs=pl.BlockSpec((1, tm, d), lambda bi, i: (bi, i, 0)),
        out_shape=jax.ShapeDtypeStruct(h.shape, F32),
        compiler_params=_params("parallel", "arbitrary"),
        name="attn_out_proj",
    )(h, a, mod3, w)


def kernel(x, c, mod_w, mod_b, norm_g, ffn_w_gate, ffn_w_up, ffn_w_down, hy_w_in, hy_w_out, gm_v_norm_g, gm_w_s, gm_b_s, ssd_conv_w, ssd_conv_b, ssd_dt_bias, ssd_a_log, ssd_d, ssd_norm_g, sb_w_qkv, sb_q_norm_g, sb_k_norm_g, sb_w_o):
    depth = mod_w.shape[0]
    b, s, d = x.shape
    mod = _modulation(c, mod_w, mod_b).reshape(depth, b, N_SUB, 3, d)
    wg = ffn_w_gate.astype(BF16)
    wu = ffn_w_up.astype(BF16)
    wd = ffn_w_down.astype(BF16)
    h = x
    for layer in range(depth):
        j = layer // 2
        h = _ffn(h, mod[layer, :, 0], norm_g[layer, 0], wg[layer, 0], wu[layer, 0], wd[layer, 0], None)
        if layer % 2 == 0:
            h = _hybrid(h, mod[layer, :, 1], norm_g[layer, 1], hy_w_in[j], hy_w_out[j],
                        gm_v_norm_g[j], gm_w_s[j], gm_b_s[j], ssd_conv_w[j], ssd_conv_b[j],
                        ssd_dt_bias[j], ssd_a_log[j], ssd_d[j], ssd_norm_g[j])
        else:
            q, k, v = _qkv(h, mod[layer, :, 1], norm_g[layer, 1], sb_w_qkv[j].astype(BF16),
                           sb_q_norm_g[j], sb_k_norm_g[j])
            o = _attention(q, k, v)
            h = _proj_residual(h, o, mod[layer, :, 1], sb_w_o[j].astype(BF16))
        h = _ffn(h, mod[layer, :, 2], norm_g[layer, 2], wg[layer, 1], wu[layer, 1], wd[layer, 1],
                 norm_g[layer, 3])
    return h
```

```python
import functools
import math

import numpy as np
import jax
import jax.numpy as jnp
from jax import lax
from jax.experimental import pallas as pl
from jax.experimental.pallas import tpu as pltpu

F32 = jnp.float32
BF16 = jnp.bfloat16
HIGHEST = lax.Precision.HIGHEST

LANES = 128
VMEM_LIMIT = 56 * 1024 * 1024

RMS_EPS = 1e-6
N_SUB = 3
GM_BLOCK = 128
GM_GROUPS = 8
CHUNK = 64
SSD_HEAD_DIM = 64
SSD_HEADS = 16
SSD_GROUPS = 2
SSD_STATE = 128
SSD_CONV = 4
SSD_L = 128
SB_HEAD_DIM = 64
SB_EXP_FLOOR = -105.0
CONV_HALO = 8

FFN_TM = 1024
FFN_FC = 256
HYB_TM = 256
QKV_TM = 512
ATT_T = 256
ATT_PAIRS = 4


def _silu(x):
    return x / (1.0 + jnp.exp(-x))


def _softplus(x):
    return jnp.maximum(x, 0.0) + jnp.log1p(jnp.exp(-jnp.abs(x)))


def _gelu_tanh(x):
    c = math.sqrt(2.0 / math.pi)
    return x * (0.5 * (1.0 + jnp.tanh(c * (x + 0.044715 * (x * x * x)))))


def _rms(x, g):
    ms = jnp.mean(x * x, axis=-1, keepdims=True)
    return x * lax.rsqrt(ms + RMS_EPS) * g


def _bf16_split(x):
    hi = x.astype(BF16).astype(F32)
    return hi, x - hi


def _norm_mod(x, g, shift, scale):
    return _rms(x, g) * (1.0 + scale) + shift


def _const_spec(shape):
    nd = len(shape)
    return pl.BlockSpec(shape, lambda *_: (0,) * nd, pipeline_mode=pl.Buffered(1))


def _params(*sem):
    return pltpu.CompilerParams(dimension_semantics=sem, vmem_limit_bytes=VMEM_LIMIT)


def _mod_kernel(c_ref, w_ref, b_ref, o_ref):
    cond = _silu(c_ref[...])
    o_ref[0] = jnp.dot(cond, w_ref[0], precision=HIGHEST, preferred_element_type=F32) + b_ref[0]


def _modulation(c, mod_w, mod_b):
    depth, d, width = mod_w.shape
    b = c.shape[0]
    tn = 1152
    return pl.pallas_call(
        _mod_kernel,
        grid=(depth, width // tn),
        in_specs=[
            pl.BlockSpec((b, d), lambda l, j: (0, 0)),
            pl.BlockSpec((1, d, tn), lambda l, j: (l, 0, j)),
            pl.BlockSpec((1, 1, tn), lambda l, j: (l, 0, j)),
        ],
        out_specs=pl.BlockSpec((1, b, tn), lambda l, j: (l, 0, j)),
        out_shape=jax.ShapeDtypeStruct((depth, b, width), F32),
        compiler_params=_params("arbitrary", "arbitrary"),
        name="modulation",
    )(c, mod_w, mod_b.reshape(depth, 1, width))


def _ffn_kernel(h_ref, mod_ref, g_ref, wg_ref, wu_ref, wd_ref, fg_ref, *rest, final_norm,
                pre_proj):
    o_ref = rest[-1]
    x = h_ref[0]
    if pre_proj:
        a_ref, pmod_ref, wo_ref = rest[:3]
        x = x + pmod_ref[0, 2:3, :] * jnp.dot(a_ref[0], wo_ref[...], preferred_element_type=F32)
    y = _norm_mod(x, g_ref[...], mod_ref[0, 0:1, :], mod_ref[0, 1:2, :]).astype(BF16)
    acc = jnp.zeros(x.shape, F32)
    for f0 in range(0, wg_ref.shape[1], FFN_FC):
        a = jnp.dot(y, wg_ref[:, f0:f0 + FFN_FC], preferred_element_type=F32)
        b = jnp.dot(y, wu_ref[:, f0:f0 + FFN_FC], preferred_element_type=F32)
        hid = (_silu(a) * b).astype(BF16)
        acc = acc + jnp.dot(hid, wd_ref[f0:f0 + FFN_FC, :], preferred_element_type=F32)
    out = x + 0.5 * mod_ref[0, 2:3, :] * acc
    if final_norm:
        out = _rms(out, fg_ref[...])
    o_ref[0] = out


def _ffn(h, mod3, g, wg, wu, wd, widx, final_g, pre=None):
    b, s, d = h.shape
    f = wg.shape[3]
    tm = min(FFN_TM, s)
    final_norm = final_g is not None
    fg = final_g if final_norm else g
    tok = pl.BlockSpec((1, tm, d), lambda bi, i: (bi, i, 0))
    mod_spec = pl.BlockSpec((1, N_SUB, d), lambda bi, i: (bi, 0, 0))

    def w_spec(rows, cols):
        return pl.BlockSpec((None, None, rows, cols), lambda bi, i: (*widx, 0, 0),
                            pipeline_mode=pl.Buffered(1))

    in_specs = [tok, mod_spec, _const_spec((1, d)), w_spec(d, f), w_spec(d, f), w_spec(f, d),
                _const_spec((1, d))]
    operands = [h, mod3, g.reshape(1, d), wg, wu, wd, fg.reshape(1, d)]
    if pre is not None:
        a, pmod3, w_o = pre
        in_specs += [pl.BlockSpec((1, tm, a.shape[2]), lambda bi, i: (bi, i, 0)), mod_spec,
                     _const_spec(w_o.shape)]
        operands += [a, pmod3, w_o]
    return pl.pallas_call(
        functools.partial(_ffn_kernel, final_norm=final_norm, pre_proj=pre is not None),
        grid=(b, s // tm),
        in_specs=in_specs,
        out_specs=tok,
        out_shape=jax.ShapeDtypeStruct(h.shape, F32),
        compiler_params=_params("parallel", "arbitrary"),
        name="ffn_proj" if pre is not None else "ffn",
    )(*operands)


def _hyb_kernel(h_ref, mod_ref, g_ref, win_ref, wdt_ref, wout_ref,
                vng_ref, ws_ref, bs_ref, convw_ref, convb_ref, dtb_ref, alog_ref, dexp_ref,
                sng_ref, emat_ref, tril_ref, o_ref, conv_buf, state_ref):
    tm = h_ref.shape[1]
    gw = vng_ref.shape[1]
    sw = sng_ref.shape[1]
    o1 = 2 * gw
    o2 = o1 + sw
    o3 = o2 + conv_buf.shape[1]
    gs = SSD_GROUPS * SSD_STATE
    hpg = SSD_HEADS // SSD_GROUPS
    gcols = hpg * SSD_HEAD_DIM

    @pl.when(pl.program_id(1) == 0)
    def _():
        conv_buf[0:CONV_HALO, :] = jnp.zeros((CONV_HALO, conv_buf.shape[1]), F32)
        state_ref[...] = jnp.zeros(state_ref.shape, F32)

    x = h_ref[0]
    y = _norm_mod(x, g_ref[...], mod_ref[0, 0:1, :], mod_ref[0, 1:2, :]).astype(BF16)

    gm = _gelu_tanh(jnp.dot(y, win_ref[:, 0:o1], preferred_element_type=F32))
    u = gm[:, :gw]
    v = gm[:, gw:]
    pos_t = lax.broadcasted_iota(jnp.int32, (GM_BLOCK, GM_BLOCK), 0)
    pos_s = lax.broadcasted_iota(jnp.int32, (GM_BLOCK, GM_BLOCK), 1)
    blk_causal = (pos_s // CHUNK) <= (pos_t // CHUNK)
    ya_cols = []
    for g in range(GM_GROUPS):
        c0 = g * LANES
        vn = _rms(v[:, c0:c0 + LANES], vng_ref[:, c0:c0 + LANES]).astype(BF16)
        wm = jnp.where(blk_causal, ws_ref[g], 0.0).astype(BF16)
        rows = []
        for r0 in range(0, tm, GM_BLOCK):
            mixed = jnp.dot(wm, vn[r0:r0 + GM_BLOCK, :], preferred_element_type=F32)
            rows.append(mixed + bs_ref[:, c0:c0 + LANES])
        ya_cols.append(u[:, c0:c0 + LANES] * jnp.concatenate(rows, axis=0))
    y_a = jnp.concatenate(ya_cols, axis=1).astype(BF16)

    z = jnp.dot(y, win_ref[:, o1:o2], preferred_element_type=F32)
    conv_buf[CONV_HALO:CONV_HALO + tm, :] = jnp.dot(y, win_ref[:, o2:o3], preferred_element_type=F32)
    dt = _softplus(jnp.dot(y, wdt_ref[...], preferred_element_type=F32) + dtb_ref[...])
    acc = convb_ref[...]
    for k in range(SSD_CONV):
        start = CONV_HALO - (SSD_CONV - 1) + k
        acc = acc + conv_buf[start:start + tm, :] * convw_ref[k:k + 1, :]
    conv_buf[0:CONV_HALO, :] = conv_buf[tm:tm + CONV_HALO, :]
    xbc = _silu(acc)
    xs = xbc[:, :sw]
    bm = xbc[:, sw:sw + gs]
    cm = xbc[:, sw + gs:sw + 2 * gs]
    dta = dt * (-jnp.exp(alog_ref[...]))

    row_i = lax.broadcasted_iota(jnp.int32, (SSD_L, SSD_L), 0)
    col_i = lax.broadcasted_iota(jnp.int32, (SSD_L, SSD_L), 1)
    causal = row_i >= col_i
    lane = lax.broadcasted_iota(jnp.int32, (SSD_L, LANES), 1)
    first_head = lane < SSD_HEAD_DIM
    yb_rows = []
    for r0 in range(0, tm, SSD_L):
        sl = slice(r0, r0 + SSD_L)
        dt_c = dt[sl]
        t0, rest = _bf16_split(dta[sl])
        t1, rest = _bf16_split(rest)
        terms = jnp.concatenate([t0, t1, rest], axis=1).astype(BF16)
        sums = jnp.dot(tril_ref[...], terms, preferred_element_type=F32)
        cs = sums[:, :LANES] + sums[:, LANES:2 * LANES] + sums[:, 2 * LANES:]
        cs_end = cs[SSD_L - 1:SSD_L, :]
        cs_t = cs.T
        dt_t = dt_c.T
        p1h, p1l = _bf16_split(jnp.exp(cs))
        p2h, p2l = _bf16_split(jnp.exp(cs_end - cs) * dt_c)
        packed = jnp.where(
            lane < SSD_HEADS, p1h,
            jnp.where(lane < 2 * SSD_HEADS, pltpu.roll(p1l, SSD_HEADS, 1),
                      jnp.where(lane < 3 * SSD_HEADS, pltpu.roll(p2h, 2 * SSD_HEADS, 1),
                                pltpu.roll(p2l, 3 * SSD_HEADS, 1))))
        spread = jnp.dot(packed.astype(BF16), emat_ref[...], preferred_element_type=F32)
        grow = spread[:, :sw]
        gend = spread[:, sw:]
        xs_c = xs[sl]
        xs_b = xs_c.astype(BF16)
        xd = (xs_c * gend).astype(BF16)
        ycols = []
        for g in range(SSD_GROUPS):
            bm_g = bm[sl, g * SSD_STATE:(g + 1) * SSD_STATE]
            cm_g = cm[sl, g * SSD_STATE:(g + 1) * SSD_STATE].astype(BF16)
            gc = slice(g * gcols, (g + 1) * gcols)
            cb = lax.dot_general(cm_g, bm_g.astype(BF16), (((1,), (1,)), ((), ())),
                                 preferred_element_type=F32)
            st = state_ref[g]
            y_off = jnp.dot(cm_g, st.astype(BF16), preferred_element_type=F32) * grow[:, gc]
            state_ref[g] = st * grow[SSD_L - 1:SSD_L, gc] + jnp.dot(
                bm_g.T.astype(BF16), xd[:, gc], preferred_element_type=F32)
            for p in range(hpg // 2):
                ws = []
                for hd in (g * hpg + 2 * p, g * hpg + 2 * p + 1):
                    seg = cs[:, hd:hd + 1] - cs_t[hd:hd + 1, :]
                    decay = jnp.where(causal, jnp.exp(jnp.where(causal, seg, 0.0)), 0.0)
                    ws.append((cb * decay * dt_t[hd:hd + 1, :]).astype(BF16))
                c0 = (g * hpg + 2 * p) * SSD_HEAD_DIM
                xp = xs_b[:, c0:c0 + LANES]
                zero = jnp.zeros_like(xp)
                rhs = jnp.concatenate([jnp.where(first_head, xp, zero),
                                       jnp.where(first_head, zero, xp)], axis=0)
                y_diag = jnp.dot(jnp.concatenate(ws, axis=1), rhs, preferred_element_type=F32)
                ycols.append(y_diag + y_off[:, 2 * p * SSD_HEAD_DIM:2 * p * SSD_HEAD_DIM + LANES])
        y_c = jnp.concatenate(ycols, axis=1) + dexp_ref[...] * xs_c
        yb_rows.append(_rms(y_c * _silu(z[sl]), sng_ref[...]))
    y_b = jnp.concatenate(yb_rows, axis=0).astype(BF16)

    mix = (jnp.dot(y_a, wout_ref[0:gw, :], preferred_element_type=F32)
           + jnp.dot(y_b, wout_ref[gw:gw + sw, :], preferred_element_type=F32))
    o_ref[0] = x + mod_ref[0, 2:3, :] * mix


def _hybrid(h, mod3, g, w_in, w_out, v_norm_g, w_s, b_s, conv_w, conv_b, dt_bias, a_log, d_skip,
            ssd_norm_g):
    b, s, d = h.shape
    gw = v_norm_g.shape[0]
    sw = ssd_norm_g.shape[0]
    cdim = conv_w.shape[1]
    tm = min(HYB_TM, s)
    o3 = 2 * gw + sw + cdim
    pad = LANES - SSD_HEADS
    w_in_b = w_in.astype(BF16)
    w_dt = jnp.pad(w_in_b[:, o3:], ((0, 0), (0, pad)))
    dtb = jnp.pad(dt_bias, (0, pad)).reshape(1, LANES)
    alog = jnp.pad(a_log, (0, pad)).reshape(1, LANES)
    dexp = jnp.repeat(d_skip, SSD_HEAD_DIM).reshape(1, sw)
    bs_full = jnp.repeat(b_s.T, gw // GM_GROUPS, axis=1)
    emat = np.zeros((LANES, 2 * sw), np.float32)
    for part in range(4):
        for hd in range(SSD_HEADS):
            c0 = (part // 2) * sw + hd * SSD_HEAD_DIM
            emat[part * SSD_HEADS + hd, c0:c0 + SSD_HEAD_DIM] = 1.0
    tril = np.tril(np.ones((SSD_L, SSD_L), np.float32))
    tok = pl.BlockSpec((1, tm, d), lambda bi, i: (bi, i, 0))
    operands = [
        g.reshape(1, d), w_in_b, w_dt,
        w_out.astype(BF16), v_norm_g.reshape(1, gw), w_s, bs_full, conv_w,
        conv_b.reshape(1, cdim), dtb, alog, dexp, ssd_norm_g.reshape(1, sw),
        jnp.asarray(emat, BF16), jnp.asarray(tril, BF16),
    ]
    return pl.pallas_call(
        _hyb_kernel,
        grid=(b, s // tm),
        in_specs=[tok, pl.BlockSpec((1, N_SUB, d), lambda bi, i: (bi, 0, 0))]
        + [_const_spec(op.shape) for op in operands],
        out_specs=tok,
        out_shape=jax.ShapeDtypeStruct(h.shape, F32),
        scratch_shapes=[
            pltpu.VMEM((tm + CONV_HALO, cdim), F32),
            pltpu.VMEM((SSD_GROUPS, SSD_STATE, sw // SSD_GROUPS), F32),
        ],
        compiler_params=_params("parallel", "arbitrary"),
        name="hybrid_mixer",
    )(h, mod3, *operands)


def _pair_rms(x, g, first_head):
    xx = x * x
    s0 = jnp.sum(jnp.where(first_head, xx, 0.0), axis=-1, keepdims=True)
    s1 = jnp.sum(jnp.where(first_head, 0.0, xx), axis=-1, keepdims=True)
    ms = jnp.where(first_head, s0, s1) * (1.0 / SB_HEAD_DIM)
    return x * lax.rsqrt(ms + RMS_EPS) * g


def _qkv_kernel(h_ref, mod_ref, g_ref, w_ref, qg_ref, kg_ref, q_ref, k_ref, v_ref):
    y = _norm_mod(h_ref[0], g_ref[...], mod_ref[0, 0:1, :], mod_ref[0, 1:2, :]).astype(BF16)
    width = q_ref.shape[2]
    first_head = lax.broadcasted_iota(jnp.int32, (1, LANES), 1) < SB_HEAD_DIM
    for part, (out, gain) in enumerate(((q_ref, qg_ref[...] * (SB_HEAD_DIM ** -0.5)),
                                        (k_ref, kg_ref[...]))):
        a = jnp.dot(y, w_ref[:, part * width:(part + 1) * width], preferred_element_type=F32)
        for c0 in range(0, width, LANES):
            out[0, :, c0:c0 + LANES] = _pair_rms(a[:, c0:c0 + LANES], gain, first_head).astype(BF16)
    v_ref[0] = jnp.dot(y, w_ref[:, 2 * width:3 * width], preferred_element_type=F32).astype(BF16)


def _qkv(h, mod3, g, w, q_norm_g, k_norm_g):
    b, s, d = h.shape
    width = w.shape[1] // 3
    tm = min(QKV_TM, s)
    out = pl.BlockSpec((1, tm, width), lambda bi, i: (bi, i, 0))
    return pl.pallas_call(
        _qkv_kernel,
        grid=(b, s // tm),
        in_specs=[
            pl.BlockSpec((1, tm, d), lambda bi, i: (bi, i, 0)),
            pl.BlockSpec((1, N_SUB, d), lambda bi, i: (bi, 0, 0)),
            _const_spec((1, d)),
            _const_spec((d, 3 * width)),
            _const_spec((1, LANES)),
            _const_spec((1, LANES)),
        ],
        out_specs=[out, out, out],
        out_shape=[jax.ShapeDtypeStruct((b, s, width), BF16)] * 3,
        compiler_params=_params("parallel", "arbitrary"),
        name="qkv_proj",
    )(h, mod3, g.reshape(1, d), w, jnp.tile(q_norm_g, 2).reshape(1, LANES),
      jnp.tile(k_norm_g, 2).reshape(1, LANES))


def _sb_tile(q_h, k_t, v_t, u, carry, mask):
    z = lax.dot_general(q_h, k_t, (((1,), (1,)), ((), ())), preferred_element_type=F32)
    log_beta = jnp.minimum(z, 0.0) - jnp.log(1.0 + jnp.exp(-jnp.abs(z)))
    log_keep = log_beta - z
    if mask is not None:
        log_keep = jnp.where(mask, log_keep, 0.0)
    hi = log_keep.astype(BF16)
    lo = (log_keep - hi.astype(F32)).astype(BF16)
    after = jnp.dot(jnp.concatenate([hi, lo], axis=1), u, preferred_element_type=F32)
    if carry is not None:
        after = after + carry
    wgt = jnp.exp(log_beta + after)
    if mask is not None:
        wgt = jnp.where(mask, wgt, 0.0)
    pv = jnp.dot(wgt.astype(BF16), v_t, preferred_element_type=F32)
    return pv, jnp.sum(log_keep, axis=-1, keepdims=True)


def _attn_kernel(q_ref, k_ref, v_ref, u_ref, o_ref, acc_ref, carry_ref):
    t = q_ref.shape[1]
    npair = q_ref.shape[2] // LANES
    i = pl.program_id(2)
    first_head = lax.broadcasted_iota(jnp.int32, (1, LANES), 1) < SB_HEAD_DIM
    u = u_ref[...]
    diag = (lax.broadcasted_iota(jnp.int32, (t, t), 1) < lax.broadcasted_iota(jnp.int32, (t, t), 0))
    diag2 = jnp.concatenate([diag, diag], axis=0)
    lanes = [slice(p * LANES, (p + 1) * LANES) for p in range(npair)]
    q2 = []
    for p in range(npair):
        q = q_ref[0, :, lanes[p]]
        zero = jnp.zeros_like(q)
        q2.append(jnp.concatenate([jnp.where(first_head, q, zero),
                                   jnp.where(first_head, zero, q)], axis=0))

    def kv_tile(j, p):
        k0 = pl.multiple_of(j * t, t)
        return k_ref[0, pl.ds(k0, t), lanes[p]], v_ref[0, pl.ds(k0, t), lanes[p]]

    def store_out():
        for p in range(npair):
            o_ref[0, :, lanes[p]] = jnp.where(first_head, acc_ref[p, 0:t, :],
                                              acc_ref[p, t:2 * t, :]).astype(o_ref.dtype)

    @pl.when(i == 0)
    def _():
        for p in range(npair):
            k_d, v_d = kv_tile(i, p)
            acc_ref[p] = _sb_tile(q2[p], k_d, v_d, u, None, diag2)[0]
        store_out()

    @pl.when(i > 0)
    def _():
        worst = None
        for p in range(npair):
            k_d, v_d = kv_tile(i, p)
            k_p, v_p = kv_tile(i - 1, p)
            pv_d, c_d = _sb_tile(q2[p], k_d, v_d, u, None, diag2)
            pv_p, c_p = _sb_tile(q2[p], k_p, v_p, u, c_d, None)
            acc_ref[p] = pv_d + pv_p
            c = c_d + c_p
            carry_ref[p] = jnp.broadcast_to(c, (2 * t, LANES))
            worst = c if worst is None else jnp.maximum(worst, c)

        def cond(state):
            j, top = state
            return jnp.logical_and(j >= 0, top > SB_EXP_FLOOR)

        def body(state):
            j, _ = state
            worst = None
            for p in range(npair):
                k_j, v_j = kv_tile(j, p)
                c = carry_ref[p, :, 0:1]
                pv, rs = _sb_tile(q2[p], k_j, v_j, u, c, None)
                acc_ref[p] += pv
                c = c + rs
                carry_ref[p] = jnp.broadcast_to(c, (2 * t, LANES))
                worst = c if worst is None else jnp.maximum(worst, c)
            return j - 1, jnp.max(worst)

        lax.while_loop(cond, body, (i - 2, jnp.max(worst)))
        store_out()


def _attention(q, k, v):
    b, s, width = q.shape
    t = min(ATT_T, s)
    gw = ATT_PAIRS * LANES
    umat = np.tril(np.ones((t, t), np.float32), -1)
    umat = np.concatenate([umat, umat], axis=0)
    q_spec = pl.BlockSpec((1, t, gw), lambda bi, p, i: (bi, i, p))
    kv_spec = pl.BlockSpec((1, s, gw), lambda bi, p, i: (bi, 0, p))
    return pl.pallas_call(
        _attn_kernel,
        grid=(b, width // gw, s // t),
        in_specs=[q_spec, kv_spec, kv_spec, _const_spec((2 * t, t))],
        out_specs=q_spec,
        out_shape=jax.ShapeDtypeStruct((b, s, width), BF16),
        scratch_shapes=[
            pltpu.VMEM((ATT_PAIRS, 2 * t, LANES), F32),
            pltpu.VMEM((ATT_PAIRS, 2 * t, LANES), F32),
        ],
        compiler_params=_params("parallel", "parallel", "arbitrary"),
        name="stickbreaking_attention",
    )(q, k, v, jnp.asarray(umat, BF16))


def kernel(x, c, mod_w, mod_b, norm_g, ffn_w_gate, ffn_w_up, ffn_w_down, hy_w_in, hy_w_out, gm_v_norm_g, gm_w_s, gm_b_s, ssd_conv_w, ssd_conv_b, ssd_dt_bias, ssd_a_log, ssd_d, ssd_norm_g, sb_w_qkv, sb_q_norm_g, sb_k_norm_g, sb_w_o):
    depth = mod_w.shape[0]
    b, s, d = x.shape
    mod = _modulation(c, mod_w, mod_b).reshape(depth, b, N_SUB, 3, d)
    wg = ffn_w_gate.astype(BF16)
    wu = ffn_w_up.astype(BF16)
    wd = ffn_w_down.astype(BF16)
    h = x
    for layer in range(depth):
        j = layer // 2
        h = _ffn(h, mod[layer, :, 0], norm_g[layer, 0], wg, wu, wd, (layer, 0), None)
        pre = None
        if layer % 2 == 0:
            h = _hybrid(h, mod[layer, :, 1], norm_g[layer, 1], hy_w_in[j], hy_w_out[j],
                        gm_v_norm_g[j], gm_w_s[j], gm_b_s[j], ssd_conv_w[j], ssd_conv_b[j],
                        ssd_dt_bias[j], ssd_a_log[j], ssd_d[j], ssd_norm_g[j])
        else:
            q, k, v = _qkv(h, mod[layer, :, 1], norm_g[layer, 1], sb_w_qkv[j].astype(BF16),
                           sb_q_norm_g[j], sb_k_norm_g[j])
            pre = (_attention(q, k, v), mod[layer, :, 1], sb_w_o[j].astype(BF16))
        h = _ffn(h, mod[layer, :, 2], norm_g[layer, 2], wg, wu, wd, (layer, 1), norm_g[layer, 3],
                 pre)
    return h
```

```python
import functools
import math

import numpy as np
import jax
import jax.numpy as jnp
from jax import lax
from jax.experimental import pallas as pl
from jax.experimental.pallas import tpu as pltpu

F32 = jnp.float32
BF16 = jnp.bfloat16
HIGHEST = lax.Precision.HIGHEST

LANES = 128
VMEM_LIMIT = 56 * 1024 * 1024

RMS_EPS = 1e-6
N_SUB = 3
GM_BLOCK = 128
GM_GROUPS = 8
CHUNK = 64
SSD_HEAD_DIM = 64
SSD_HEADS = 16
SSD_GROUPS = 2
SSD_STATE = 128
SSD_CONV = 4
SSD_L = 128
SB_HEAD_DIM = 64
SB_EXP_FLOOR = -105.0
CONV_HALO = 8

MOD_TN = 2304
FFN_TM = 1024
FFN_FC = 256
HYB_TM = 512
QKV_TM = 1024
ATT_T = 256
ATT_PAIRS = 8


def _silu(x):
    return x / (1.0 + jnp.exp(-x))


def _softplus(x):
    return jnp.maximum(x, 0.0) + jnp.log1p(jnp.exp(-jnp.abs(x)))


def _gelu_tanh(x):
    c = math.sqrt(2.0 / math.pi)
    return x * (0.5 * (1.0 + jnp.tanh(c * (x + 0.044715 * (x * x * x)))))


def _rms(x, g):
    ms = jnp.mean(x * x, axis=-1, keepdims=True)
    return x * lax.rsqrt(ms + RMS_EPS) * g


def _bf16_split(x):
    hi = x.astype(BF16).astype(F32)
    return hi, x - hi


def _norm_mod(x, g, shift, scale):
    return _rms(x, g) * (1.0 + scale) + shift


def _const_spec(shape):
    nd = len(shape)
    return pl.BlockSpec(shape, lambda *_: (0,) * nd, pipeline_mode=pl.Buffered(1))


def _params(*sem):
    return pltpu.CompilerParams(dimension_semantics=sem, vmem_limit_bytes=VMEM_LIMIT)


def _mod_kernel(c_ref, w_ref, b_ref, o_ref):
    cond = _silu(c_ref[...])
    o_ref[0] = jnp.dot(cond, w_ref[0], precision=HIGHEST, preferred_element_type=F32) + b_ref[0]


def _modulation(c, mod_w, mod_b):
    depth, d, width = mod_w.shape
    b = c.shape[0]
    tn = MOD_TN
    return pl.pallas_call(
        _mod_kernel,
        grid=(depth, width // tn),
        in_specs=[
            pl.BlockSpec((b, d), lambda l, j: (0, 0)),
            pl.BlockSpec((1, d, tn), lambda l, j: (l, 0, j)),
            pl.BlockSpec((1, 1, tn), lambda l, j: (l, 0, j)),
        ],
        out_specs=pl.BlockSpec((1, b, tn), lambda l, j: (l, 0, j)),
        out_shape=jax.ShapeDtypeStruct((depth, b, width), F32),
        compiler_params=_params("arbitrary", "arbitrary"),
        name="modulation",
    )(c, mod_w, mod_b.reshape(depth, 1, width))


def _ffn_kernel(h_ref, mod_ref, g_ref, wg_ref, wu_ref, wd_ref, fg_ref, *rest, final_norm,
                pre_proj):
    o_ref = rest[-1]
    x = h_ref[0]
    if pre_proj:
        a_ref, pmod_ref, wo_ref = rest[:3]
        x = x + pmod_ref[0, 2:3, :] * jnp.dot(a_ref[0], wo_ref[...], preferred_element_type=F32)
    y = _norm_mod(x, g_ref[...], mod_ref[0, 0:1, :], mod_ref[0, 1:2, :]).astype(BF16)
    acc = jnp.zeros(x.shape, F32)
    for f0 in range(0, wg_ref.shape[1], FFN_FC):
        a = jnp.dot(y, wg_ref[:, f0:f0 + FFN_FC], preferred_element_type=F32)
        b = jnp.dot(y, wu_ref[:, f0:f0 + FFN_FC], preferred_element_type=F32)
        hid = (_silu(a) * b).astype(BF16)
        acc = acc + jnp.dot(hid, wd_ref[f0:f0 + FFN_FC, :], preferred_element_type=F32)
    out = x + 0.5 * mod_ref[0, 2:3, :] * acc
    if final_norm:
        out = _rms(out, fg_ref[...])
    o_ref[0] = out


def _ffn(h, mod3, g, wg, wu, wd, widx, final_g, pre=None):
    b, s, d = h.shape
    f = wg.shape[3]
    tm = min(FFN_TM, s)
    final_norm = final_g is not None
    fg = final_g if final_norm else g
    tok = pl.BlockSpec((1, tm, d), lambda bi, i: (bi, i, 0))
    mod_spec = pl.BlockSpec((1, N_SUB, d), lambda bi, i: (bi, 0, 0))

    def w_spec(rows, cols):
        return pl.BlockSpec((None, None, rows, cols), lambda bi, i: (*widx, 0, 0),
                            pipeline_mode=pl.Buffered(1))

    in_specs = [tok, mod_spec, _const_spec((1, d)), w_spec(d, f), w_spec(d, f), w_spec(f, d),
                _const_spec((1, d))]
    operands = [h, mod3, g.reshape(1, d), wg, wu, wd, fg.reshape(1, d)]
    if pre is not None:
        a, pmod3, w_o = pre
        in_specs += [pl.BlockSpec((1, tm, a.shape[2]), lambda bi, i: (bi, i, 0)), mod_spec,
                     _const_spec(w_o.shape)]
        operands += [a, pmod3, w_o]
    return pl.pallas_call(
        functools.partial(_ffn_kernel, final_norm=final_norm, pre_proj=pre is not None),
        grid=(b, s // tm),
        in_specs=in_specs,
        out_specs=tok,
        out_shape=jax.ShapeDtypeStruct(h.shape, F32),
        compiler_params=_params("parallel", "arbitrary"),
        name="ffn_proj" if pre is not None else "ffn",
    )(*operands)


def _hyb_kernel(h_ref, mod_ref, g_ref, win_ref, wdt_ref, wout_ref,
                vng_ref, ws_ref, bs_ref, convw_ref, convb_ref, dtb_ref, alog_ref, dexp_ref,
                sng_ref, emat_ref, tril_ref, o_ref, conv_buf, state_ref):
    tm = h_ref.shape[1]
    gw = vng_ref.shape[1]
    sw = sng_ref.shape[1]
    o1 = 2 * gw
    o2 = o1 + sw
    o3 = o2 + conv_buf.shape[1]
    gs = SSD_GROUPS * SSD_STATE
    hpg = SSD_HEADS // SSD_GROUPS
    gcols = hpg * SSD_HEAD_DIM

    @pl.when(pl.program_id(1) == 0)
    def _():
        conv_buf[0:CONV_HALO, :] = jnp.zeros((CONV_HALO, conv_buf.shape[1]), F32)
        state_ref[...] = jnp.zeros(state_ref.shape, F32)

    x = h_ref[0]
    y = _norm_mod(x, g_ref[...], mod_ref[0, 0:1, :], mod_ref[0, 1:2, :]).astype(BF16)

    gm = _gelu_tanh(jnp.dot(y, win_ref[:, 0:o1], preferred_element_type=F32))
    u = gm[:, :gw]
    v = gm[:, gw:]
    pos_t = lax.broadcasted_iota(jnp.int32, (GM_BLOCK, GM_BLOCK), 0)
    pos_s = lax.broadcasted_iota(jnp.int32, (GM_BLOCK, GM_BLOCK), 1)
    blk_causal = (pos_s // CHUNK) <= (pos_t // CHUNK)
    ya_cols = []
    for g in range(GM_GROUPS):
        c0 = g * LANES
        vn = _rms(v[:, c0:c0 + LANES], vng_ref[:, c0:c0 + LANES]).astype(BF16)
        wm = jnp.where(blk_causal, ws_ref[g], 0.0).astype(BF16)
        rows = []
        for r0 in range(0, tm, GM_BLOCK):
            mixed = jnp.dot(wm, vn[r0:r0 + GM_BLOCK, :], preferred_element_type=F32)
            rows.append(mixed + bs_ref[:, c0:c0 + LANES])
        ya_cols.append(u[:, c0:c0 + LANES] * jnp.concatenate(rows, axis=0))
    y_a = jnp.concatenate(ya_cols, axis=1).astype(BF16)

    z = jnp.dot(y, win_ref[:, o1:o2], preferred_element_type=F32)
    conv_buf[CONV_HALO:CONV_HALO + tm, :] = jnp.dot(y, win_ref[:, o2:o3], preferred_element_type=F32)
    dt = _softplus(jnp.dot(y, wdt_ref[...], preferred_element_type=F32) + dtb_ref[...])
    acc = convb_ref[...]
    for k in range(SSD_CONV):
        start = CONV_HALO - (SSD_CONV - 1) + k
        acc = acc + conv_buf[start:start + tm, :] * convw_ref[k:k + 1, :]
    conv_buf[0:CONV_HALO, :] = conv_buf[tm:tm + CONV_HALO, :]
    xbc = _silu(acc)
    xs = xbc[:, :sw]
    bm = xbc[:, sw:sw + gs]
    cm = xbc[:, sw + gs:sw + 2 * gs]
    dta = dt * (-jnp.exp(alog_ref[...]))

    row_i = lax.broadcasted_iota(jnp.int32, (SSD_L, SSD_L), 0)
    col_i = lax.broadcasted_iota(jnp.int32, (SSD_L, SSD_L), 1)
    causal = row_i >= col_i
    lane = lax.broadcasted_iota(jnp.int32, (SSD_L, LANES), 1)
    first_head = lane < SSD_HEAD_DIM
    yb_rows = []
    for r0 in range(0, tm, SSD_L):
        sl = slice(r0, r0 + SSD_L)
        dt_c = dt[sl]
        t0, rest = _bf16_split(dta[sl])
        t1, rest = _bf16_split(rest)
        terms = jnp.concatenate([t0, t1, rest], axis=1).astype(BF16)
        sums = jnp.dot(tril_ref[...], terms, preferred_element_type=F32)
        cs = sums[:, :LANES] + sums[:, LANES:2 * LANES] + sums[:, 2 * LANES:]
        cs_end = cs[SSD_L - 1:SSD_L, :]
        cs_t = cs.T
        dt_t = dt_c.T
        p1h, p1l = _bf16_split(jnp.exp(cs))
        p2h, p2l = _bf16_split(jnp.exp(cs_end - cs) * dt_c)
        packed = jnp.where(
            lane < SSD_HEADS, p1h,
            jnp.where(lane < 2 * SSD_HEADS, pltpu.roll(p1l, SSD_HEADS, 1),
                      jnp.where(lane < 3 * SSD_HEADS, pltpu.roll(p2h, 2 * SSD_HEADS, 1),
                                pltpu.roll(p2l, 3 * SSD_HEADS, 1))))
        spread = jnp.dot(packed.astype(BF16), emat_ref[...], preferred_element_type=F32)
        grow = spread[:, :sw]
        gend = spread[:, sw:]
        xs_c = xs[sl]
        xs_b = xs_c.astype(BF16)
        xd = (xs_c * gend).astype(BF16)
        ycols = []
        for g in range(SSD_GROUPS):
            bm_g = bm[sl, g * SSD_STATE:(g + 1) * SSD_STATE]
            cm_g = cm[sl, g * SSD_STATE:(g + 1) * SSD_STATE].astype(BF16)
            gc = slice(g * gcols, (g + 1) * gcols)
            cb = lax.dot_general(cm_g, bm_g.astype(BF16), (((1,), (1,)), ((), ())),
                                 preferred_element_type=F32)
            st = state_ref[g]
            y_off = jnp.dot(cm_g, st.astype(BF16), preferred_element_type=F32) * grow[:, gc]
            state_ref[g] = st * grow[SSD_L - 1:SSD_L, gc] + jnp.dot(
                bm_g.T.astype(BF16), xd[:, gc], preferred_element_type=F32)
            for p in range(hpg // 2):
                ws = []
                for hd in (g * hpg + 2 * p, g * hpg + 2 * p + 1):
                    seg = cs[:, hd:hd + 1] - cs_t[hd:hd + 1, :]
                    decay = jnp.where(causal, jnp.exp(jnp.where(causal, seg, 0.0)), 0.0)
                    ws.append((cb * decay * dt_t[hd:hd + 1, :]).astype(BF16))
                c0 = (g * hpg + 2 * p) * SSD_HEAD_DIM
                xp = xs_b[:, c0:c0 + LANES]
                zero = jnp.zeros_like(xp)
                rhs = jnp.concatenate([jnp.where(first_head, xp, zero),
                                       jnp.where(first_head, zero, xp)], axis=0)
                y_diag = jnp.dot(jnp.concatenate(ws, axis=1), rhs, preferred_element_type=F32)
                ycols.append(y_diag + y_off[:, 2 * p * SSD_HEAD_DIM:2 * p * SSD_HEAD_DIM + LANES])
        y_c = jnp.concatenate(ycols, axis=1) + dexp_ref[...] * xs_c
        yb_rows.append(_rms(y_c * _silu(z[sl]), sng_ref[...]))
    y_b = jnp.concatenate(yb_rows, axis=0).astype(BF16)

    mix = (jnp.dot(y_a, wout_ref[0:gw, :], preferred_element_type=F32)
           + jnp.dot(y_b, wout_ref[gw:gw + sw, :], preferred_element_type=F32))
    o_ref[0] = x + mod_ref[0, 2:3, :] * mix


def _hybrid(h, mod3, g, w_in, w_out, v_norm_g, w_s, b_s, conv_w, conv_b, dt_bias, a_log, d_skip,
            ssd_norm_g):
    b, s, d = h.shape
    gw = v_norm_g.shape[0]
    sw = ssd_norm_g.shape[0]
    cdim = conv_w.shape[1]
    tm = min(HYB_TM, s)
    o3 = 2 * gw + sw + cdim
    pad = LANES - SSD_HEADS
    w_in_b = w_in.astype(BF16)
    w_dt = jnp.pad(w_in_b[:, o3:], ((0, 0), (0, pad)))
    dtb = jnp.pad(dt_bias, (0, pad)).reshape(1, LANES)
    alog = jnp.pad(a_log, (0, pad)).reshape(1, LANES)
    dexp = jnp.repeat(d_skip, SSD_HEAD_DIM).reshape(1, sw)
    bs_full = jnp.repeat(b_s.T, gw // GM_GROUPS, axis=1)
    emat = np.zeros((LANES, 2 * sw), np.float32)
    for part in range(4):
        for hd in range(SSD_HEADS):
            c0 = (part // 2) * sw + hd * SSD_HEAD_DIM
            emat[part * SSD_HEADS + hd, c0:c0 + SSD_HEAD_DIM] = 1.0
    tril = np.tril(np.ones((SSD_L, SSD_L), np.float32))
    tok = pl.BlockSpec((1, tm, d), lambda bi, i: (bi, i, 0))
    operands = [
        g.reshape(1, d), w_in_b, w_dt,
        w_out.astype(BF16), v_norm_g.reshape(1, gw), w_s, bs_full, conv_w,
        conv_b.reshape(1, cdim), dtb, alog, dexp, ssd_norm_g.reshape(1, sw),
        jnp.asarray(emat, BF16), jnp.asarray(tril, BF16),
    ]
    return pl.pallas_call(
        _hyb_kernel,
        grid=(b, s // tm),
        in_specs=[tok, pl.BlockSpec((1, N_SUB, d), lambda bi, i: (bi, 0, 0))]
        + [_const_spec(op.shape) for op in operands],
        out_specs=tok,
        out_shape=jax.ShapeDtypeStruct(h.shape, F32),
        scratch_shapes=[
            pltpu.VMEM((tm + CONV_HALO, cdim), F32),
            pltpu.VMEM((SSD_GROUPS, SSD_STATE, sw // SSD_GROUPS), F32),
        ],
        compiler_params=_params("parallel", "arbitrary"),
        name="hybrid_mixer",
    )(h, mod3, *operands)


def _pair_rms(x, g, first_head):
    xx = x * x
    s0 = jnp.sum(jnp.where(first_head, xx, 0.0), axis=-1, keepdims=True)
    s1 = jnp.sum(jnp.where(first_head, 0.0, xx), axis=-1, keepdims=True)
    ms = jnp.where(first_head, s0, s1) * (1.0 / SB_HEAD_DIM)
    return x * lax.rsqrt(ms + RMS_EPS) * g


def _qkv_kernel(h_ref, mod_ref, g_ref, w_ref, qg_ref, kg_ref, q_ref, k_ref, v_ref):
    y = _norm_mod(h_ref[0], g_ref[...], mod_ref[0, 0:1, :], mod_ref[0, 1:2, :]).astype(BF16)
    width = q_ref.shape[2]
    first_head = lax.broadcasted_iota(jnp.int32, (1, LANES), 1) < SB_HEAD_DIM
    for part, (out, gain) in enumerate(((q_ref, qg_ref[...] * (SB_HEAD_DIM ** -0.5)),
                                        (k_ref, kg_ref[...]))):
        a = jnp.dot(y, w_ref[:, part * width:(part + 1) * width], preferred_element_type=F32)
        for c0 in range(0, width, LANES):
            out[0, :, c0:c0 + LANES] = _pair_rms(a[:, c0:c0 + LANES], gain, first_head).astype(BF16)
    v_ref[0] = jnp.dot(y, w_ref[:, 2 * width:3 * width], preferred_element_type=F32).astype(BF16)


def _qkv(h, mod3, g, w, q_norm_g, k_norm_g):
    b, s, d = h.shape
    width = w.shape[1] // 3
    tm = min(QKV_TM, s)
    out = pl.BlockSpec((1, tm, width), lambda bi, i: (bi, i, 0))
    return pl.pallas_call(
        _qkv_kernel,
        grid=(b, s // tm),
        in_specs=[
            pl.BlockSpec((1, tm, d), lambda bi, i: (bi, i, 0)),
            pl.BlockSpec((1, N_SUB, d), lambda bi, i: (bi, 0, 0)),
            _const_spec((1, d)),
            _const_spec((d, 3 * width)),
            _const_spec((1, LANES)),
            _const_spec((1, LANES)),
        ],
        out_specs=[out, out, out],
        out_shape=[jax.ShapeDtypeStruct((b, s, width), BF16)] * 3,
        compiler_params=_params("parallel", "arbitrary"),
        name="qkv_proj",
    )(h, mod3, g.reshape(1, d), w, jnp.tile(q_norm_g, 2).reshape(1, LANES),
      jnp.tile(k_norm_g, 2).reshape(1, LANES))


def _sb_tile(q_h, k_t, v_t, u, carry, mask):
    z = lax.dot_general(q_h, k_t, (((1,), (1,)), ((), ())), preferred_element_type=F32)
    log_beta = jnp.minimum(z, 0.0) - jnp.log(1.0 + jnp.exp(-jnp.abs(z)))
    log_keep = log_beta - z
    if mask is not None:
        log_keep = jnp.where(mask, log_keep, 0.0)
    hi = log_keep.astype(BF16)
    lo = (log_keep - hi.astype(F32)).astype(BF16)
    after = jnp.dot(jnp.concatenate([hi, lo], axis=1), u, preferred_element_type=F32)
    if carry is not None:
        after = after + carry
    wgt = jnp.exp(log_beta + after)
    if mask is not None:
        wgt = jnp.where(mask, wgt, 0.0)
    pv = jnp.dot(wgt.astype(BF16), v_t, preferred_element_type=F32)
    return pv, jnp.sum(log_keep, axis=-1, keepdims=True)


def _attn_kernel(q_ref, k_ref, v_ref, u_ref, o_ref, acc_ref, carry_ref):
    t = q_ref.shape[1]
    npair = q_ref.shape[2] // LANES
    i = pl.program_id(2)
    first_head = lax.broadcasted_iota(jnp.int32, (1, LANES), 1) < SB_HEAD_DIM
    u = u_ref[...]
    diag = (lax.broadcasted_iota(jnp.int32, (t, t), 1) < lax.broadcasted_iota(jnp.int32, (t, t), 0))
    diag2 = jnp.concatenate([diag, diag], axis=0)
    lanes = [slice(p * LANES, (p + 1) * LANES) for p in range(npair)]
    q2 = []
    for p in range(npair):
        q = q_ref[0, :, lanes[p]]
        zero = jnp.zeros_like(q)
        q2.append(jnp.concatenate([jnp.where(first_head, q, zero),
                                   jnp.where(first_head, zero, q)], axis=0))

    def kv_tile(j, p):
        k0 = pl.multiple_of(j * t, t)
        return k_ref[0, pl.ds(k0, t), lanes[p]], v_ref[0, pl.ds(k0, t), lanes[p]]

    def store_out():
        for p in range(npair):
            o_ref[0, :, lanes[p]] = jnp.where(first_head, acc_ref[p, 0:t, :],
                                              acc_ref[p, t:2 * t, :]).astype(o_ref.dtype)

    @pl.when(i == 0)
    def _():
        for p in range(npair):
            k_d, v_d = kv_tile(i, p)
            acc_ref[p] = _sb_tile(q2[p], k_d, v_d, u, None, diag2)[0]
        store_out()

    @pl.when(i > 0)
    def _():
        worst = None
        for p in range(npair):
            k_d, v_d = kv_tile(i, p)
            k_p, v_p = kv_tile(i - 1, p)
            pv_d, c_d = _sb_tile(q2[p], k_d, v_d, u, None, diag2)
            pv_p, c_p = _sb_tile(q2[p], k_p, v_p, u, c_d, None)
            acc_ref[p] = pv_d + pv_p
            c = c_d + c_p
            carry_ref[p] = jnp.broadcast_to(c, (2 * t, LANES))
            worst = c if worst is None else jnp.maximum(worst, c)

        def cond(state):
            j, top = state
            return jnp.logical_and(j >= 0, top > SB_EXP_FLOOR)

        def body(state):
            j, _ = state
            worst = None
            for p in range(npair):
                k_j, v_j = kv_tile(j, p)
                c = carry_ref[p, :, 0:1]
                pv, rs = _sb_tile(q2[p], k_j, v_j, u, c, None)
                acc_ref[p] += pv
                c = c + rs
                carry_ref[p] = jnp.broadcast_to(c, (2 * t, LANES))
                worst = c if worst is None else jnp.maximum(worst, c)
            return j - 1, jnp.max(worst)

        lax.while_loop(cond, body, (i - 2, jnp.max(worst)))
        store_out()


def _attention(q, k, v):
    b, s, width = q.shape
    t = min(ATT_T, s)
    gw = ATT_PAIRS * LANES
    umat = np.tril(np.ones((t, t), np.float32), -1)
    umat = np.concatenate([umat, umat], axis=0)
    q_spec = pl.BlockSpec((1, t, gw), lambda bi, p, i: (bi, i, p))
    kv_spec = pl.BlockSpec((1, s, gw), lambda bi, p, i: (bi, 0, p))
    return pl.pallas_call(
        _attn_kernel,
        grid=(b, width // gw, s // t),
        in_specs=[q_spec, kv_spec, kv_spec, _const_spec((2 * t, t))],
        out_specs=q_spec,
        out_shape=jax.ShapeDtypeStruct((b, s, width), BF16),
        scratch_shapes=[
            pltpu.VMEM((ATT_PAIRS, 2 * t, LANES), F32),
            pltpu.VMEM((ATT_PAIRS, 2 * t, LANES), F32),
        ],
        compiler_params=_params("parallel", "parallel", "arbitrary"),
        name="stickbreaking_attention",
    )(q, k, v, jnp.asarray(umat, BF16))


def kernel(x, c, mod_w, mod_b, norm_g, ffn_w_gate, ffn_w_up, ffn_w_down, hy_w_in, hy_w_out, gm_v_norm_g, gm_w_s, gm_b_s, ssd_conv_w, ssd_conv_b, ssd_dt_bias, ssd_a_log, ssd_d, ssd_norm_g, sb_w_qkv, sb_q_norm_g, sb_k_norm_g, sb_w_o):
    depth = mod_w.shape[0]
    b, s, d = x.shape
    mod = _modulation(c, mod_w, mod_b).reshape(depth, b, N_SUB, 3, d)
    wg = ffn_w_gate.astype(BF16)
    wu = ffn_w_up.astype(BF16)
    wd = ffn_w_down.astype(BF16)
    h = x
    for layer in range(depth):
        j = layer // 2
        h = _ffn(h, mod[layer, :, 0], norm_g[layer, 0], wg, wu, wd, (layer, 0), None)
        pre = None
        if layer % 2 == 0:
            h = _hybrid(h, mod[layer, :, 1], norm_g[layer, 1], hy_w_in[j], hy_w_out[j],
                        gm_v_norm_g[j], gm_w_s[j], gm_b_s[j], ssd_conv_w[j], ssd_conv_b[j],
                        ssd_dt_bias[j], ssd_a_log[j], ssd_d[j], ssd_norm_g[j])
        else:
            q, k, v = _qkv(h, mod[layer, :, 1], norm_g[layer, 1], sb_w_qkv[j].astype(BF16),
                           sb_q_norm_g[j], sb_k_norm_g[j])
            pre = (_attention(q, k, v), mod[layer, :, 1], sb_w_o[j].astype(BF16))
        h = _ffn(h, mod[layer, :, 2], norm_g[layer, 2], wg, wu, wd, (layer, 1), norm_g[layer, 3],
                 pre)
    return h
```

```python
import functools
import math

import numpy as np
import jax
import jax.numpy as jnp
from jax import lax
from jax.experimental import pallas as pl
from jax.experimental.pallas import tpu as pltpu

F32 = jnp.float32
BF16 = jnp.bfloat16
HIGHEST = lax.Precision.HIGHEST

LANES = 128
VMEM_LIMIT = 56 * 1024 * 1024

RMS_EPS = 1e-6
N_SUB = 3
GM_BLOCK = 128
GM_GROUPS = 8
CHUNK = 64
SSD_HEAD_DIM = 64
SSD_HEADS = 16
SSD_GROUPS = 2
SSD_STATE = 128
SSD_CONV = 4
SSD_L = 128
SB_HEAD_DIM = 64
SB_EXP_FLOOR = -105.0
CONV_HALO = 8

MOD_TN = 2304
FFN_TM = 1024
FFN_FC = 256
FFN_ROW_BLOCKS = 4
HYB_TM = 512
QKV_TM = 1024
ATT_T = 256
ATT_PAIRS = 8


def _silu(x):
    return x / (1.0 + jnp.exp(-x))


def _softplus(x):
    return jnp.maximum(x, 0.0) + jnp.log1p(jnp.exp(-jnp.abs(x)))


def _gelu_tanh(x):
    c = math.sqrt(2.0 / math.pi)
    return x * (0.5 * (1.0 + jnp.tanh(c * (x + 0.044715 * (x * x * x)))))


def _rms(x, g):
    ms = jnp.mean(x * x, axis=-1, keepdims=True)
    return x * lax.rsqrt(ms + RMS_EPS) * g


def _bf16_split(x):
    hi = x.astype(BF16).astype(F32)
    return hi, x - hi


def _norm_mod(x, g, shift, scale):
    ms = jnp.mean(x * x, axis=-1, keepdims=True)
    return x * lax.rsqrt(ms + RMS_EPS) * (g * (1.0 + scale)) + shift


def _const_spec(shape):
    nd = len(shape)
    return pl.BlockSpec(shape, lambda *_: (0,) * nd, pipeline_mode=pl.Buffered(1))


def _params(*sem):
    return pltpu.CompilerParams(dimension_semantics=sem, vmem_limit_bytes=VMEM_LIMIT)


def _mod_kernel(c_ref, w_ref, b_ref, o_ref):
    cond = _silu(c_ref[...])
    o_ref[0] = jnp.dot(cond, w_ref[0], precision=HIGHEST, preferred_element_type=F32) + b_ref[0]


def _modulation(c, mod_w, mod_b):
    depth, d, width = mod_w.shape
    b = c.shape[0]
    tn = MOD_TN
    return pl.pallas_call(
        _mod_kernel,
        grid=(depth, width // tn),
        in_specs=[
            pl.BlockSpec((b, d), lambda l, j: (0, 0)),
            pl.BlockSpec((1, d, tn), lambda l, j: (l, 0, j)),
            pl.BlockSpec((1, 1, tn), lambda l, j: (l, 0, j)),
        ],
        out_specs=pl.BlockSpec((1, b, tn), lambda l, j: (l, 0, j)),
        out_shape=jax.ShapeDtypeStruct((depth, b, width), F32),
        compiler_params=_params("arbitrary", "arbitrary"),
        name="modulation",
    )(c, mod_w, mod_b.reshape(depth, 1, width))


def _ffn_kernel(h_ref, mod_ref, g_ref, wg_ref, wu_ref, wd_ref, fg_ref, *rest, final_norm,
                pre_proj):
    o_ref = rest[-1]
    x = h_ref[0]
    if pre_proj:
        a_ref, pmod_ref, wo_ref = rest[:3]
        x = x + pmod_ref[0, 2:3, :] * jnp.dot(a_ref[0], wo_ref[...], preferred_element_type=F32)
    rb = x.shape[0] // FFN_ROW_BLOCKS
    y_blocks = [
        _norm_mod(x[r0:r0 + rb], g_ref[...], mod_ref[0, 0:1, :], mod_ref[0, 1:2, :]).astype(BF16)
        for r0 in range(0, x.shape[0], rb)]
    y = jnp.concatenate(y_blocks, axis=0)
    acc = jnp.zeros(x.shape, F32)
    for f0 in range(0, wg_ref.shape[1], FFN_FC):
        lhs = y_blocks if f0 == 0 else [y]
        a = jnp.concatenate([jnp.dot(yb, wg_ref[:, f0:f0 + FFN_FC], preferred_element_type=F32)
                             for yb in lhs], axis=0)
        b = jnp.concatenate([jnp.dot(yb, wu_ref[:, f0:f0 + FFN_FC], preferred_element_type=F32)
                             for yb in lhs], axis=0)
        hid = (_silu(a) * b).astype(BF16)
        acc = acc + jnp.dot(hid, wd_ref[f0:f0 + FFN_FC, :], preferred_element_type=F32)
    out = x + 0.5 * mod_ref[0, 2:3, :] * acc
    if final_norm:
        out = _rms(out, fg_ref[...])
    o_ref[0] = out


def _ffn(h, mod3, g, wg, wu, wd, widx, final_g, pre=None):
    b, s, d = h.shape
    f = wg.shape[3]
    tm = min(FFN_TM, s)
    final_norm = final_g is not None
    fg = final_g if final_norm else g
    tok = pl.BlockSpec((1, tm, d), lambda bi, i: (bi, i, 0))
    mod_spec = pl.BlockSpec((1, N_SUB, d), lambda bi, i: (bi, 0, 0))

    def w_spec(rows, cols):
        return pl.BlockSpec((None, None, rows, cols), lambda bi, i: (*widx, 0, 0),
                            pipeline_mode=pl.Buffered(1))

    in_specs = [tok, mod_spec, _const_spec((1, d)), w_spec(d, f), w_spec(d, f), w_spec(f, d),
                _const_spec((1, d))]
    operands = [h, mod3, g.reshape(1, d), wg, wu, wd, fg.reshape(1, d)]
    if pre is not None:
        a, pmod3, w_o = pre
        in_specs += [pl.BlockSpec((1, tm, a.shape[2]), lambda bi, i: (bi, i, 0)), mod_spec,
                     _const_spec(w_o.shape)]
        operands += [a, pmod3, w_o]
    return pl.pallas_call(
        functools.partial(_ffn_kernel, final_norm=final_norm, pre_proj=pre is not None),
        grid=(b, s // tm),
        in_specs=in_specs,
        out_specs=tok,
        out_shape=jax.ShapeDtypeStruct(h.shape, F32),
        compiler_params=_params("parallel", "arbitrary"),
        name="ffn_proj" if pre is not None else "ffn",
    )(*operands)


def _hyb_kernel(h_ref, mod_ref, g_ref, win_ref, wdt_ref, wout_ref,
                vng_ref, ws_ref, bs_ref, convw_ref, convb_ref, dtb_ref, alog_ref, dexp_ref,
                sng_ref, emat_ref, tril_ref, o_ref, conv_buf, state_ref):
    tm = h_ref.shape[1]
    gw = vng_ref.shape[1]
    sw = sng_ref.shape[1]
    o1 = 2 * gw
    o2 = o1 + sw
    o3 = o2 + conv_buf.shape[1]
    gs = SSD_GROUPS * SSD_STATE
    hpg = SSD_HEADS // SSD_GROUPS
    gcols = hpg * SSD_HEAD_DIM

    @pl.when(pl.program_id(1) == 0)
    def _():
        conv_buf[0:CONV_HALO, :] = jnp.zeros((CONV_HALO, conv_buf.shape[1]), F32)
        state_ref[...] = jnp.zeros(state_ref.shape, F32)

    x = h_ref[0]
    y = _norm_mod(x, g_ref[...], mod_ref[0, 0:1, :], mod_ref[0, 1:2, :]).astype(BF16)

    gm = _gelu_tanh(jnp.dot(y, win_ref[:, 0:o1], preferred_element_type=F32))
    u = gm[:, :gw]
    v = gm[:, gw:]
    pos_t = lax.broadcasted_iota(jnp.int32, (GM_BLOCK, GM_BLOCK), 0)
    pos_s = lax.broadcasted_iota(jnp.int32, (GM_BLOCK, GM_BLOCK), 1)
    blk_causal = (pos_s // CHUNK) <= (pos_t // CHUNK)
    ya_cols = []
    for g in range(GM_GROUPS):
        c0 = g * LANES
        vn = _rms(v[:, c0:c0 + LANES], vng_ref[:, c0:c0 + LANES]).astype(BF16)
        wm = jnp.where(blk_causal, ws_ref[g], 0.0).astype(BF16)
        rows = []
        for r0 in range(0, tm, GM_BLOCK):
            mixed = jnp.dot(wm, vn[r0:r0 + GM_BLOCK, :], preferred_element_type=F32)
            rows.append(mixed + bs_ref[:, c0:c0 + LANES])
        ya_cols.append(u[:, c0:c0 + LANES] * jnp.concatenate(rows, axis=0))
    y_a = jnp.concatenate(ya_cols, axis=1).astype(BF16)

    z = jnp.dot(y, win_ref[:, o1:o2], preferred_element_type=F32)
    conv_buf[CONV_HALO:CONV_HALO + tm, :] = jnp.dot(y, win_ref[:, o2:o3], preferred_element_type=F32)
    dt = _softplus(jnp.dot(y, wdt_ref[...], preferred_element_type=F32) + dtb_ref[...])
    acc = convb_ref[...]
    for k in range(SSD_CONV):
        start = CONV_HALO - (SSD_CONV - 1) + k
        acc = acc + conv_buf[start:start + tm, :] * convw_ref[k:k + 1, :]
    conv_buf[0:CONV_HALO, :] = conv_buf[tm:tm + CONV_HALO, :]
    xbc = _silu(acc)
    xs = xbc[:, :sw]
    bm = xbc[:, sw:sw + gs]
    cm = xbc[:, sw + gs:sw + 2 * gs]
    dta = dt * (-jnp.exp(alog_ref[...]))

    row_i = lax.broadcasted_iota(jnp.int32, (SSD_L, SSD_L), 0)
    col_i = lax.broadcasted_iota(jnp.int32, (SSD_L, SSD_L), 1)
    causal = row_i >= col_i
    lane = lax.broadcasted_iota(jnp.int32, (SSD_L, LANES), 1)
    first_head = lane < SSD_HEAD_DIM
    yb_rows = []
    for r0 in range(0, tm, SSD_L):
        sl = slice(r0, r0 + SSD_L)
        dt_c = dt[sl]
        t0, rest = _bf16_split(dta[sl])
        t1, rest = _bf16_split(rest)
        terms = jnp.concatenate([t0, t1, rest], axis=1).astype(BF16)
        sums = jnp.dot(tril_ref[...], terms, preferred_element_type=F32)
        cs = sums[:, :LANES] + sums[:, LANES:2 * LANES] + sums[:, 2 * LANES:]
        cs_end = cs[SSD_L - 1:SSD_L, :]
        cs_t = cs.T
        dt_t = dt_c.T
        p1h, p1l = _bf16_split(jnp.exp(cs))
        p2h, p2l = _bf16_split(jnp.exp(cs_end - cs) * dt_c)
        packed = jnp.where(
            lane < SSD_HEADS, p1h,
            jnp.where(lane < 2 * SSD_HEADS, pltpu.roll(p1l, SSD_HEADS, 1),
                      jnp.where(lane < 3 * SSD_HEADS, pltpu.roll(p2h, 2 * SSD_HEADS, 1),
                                pltpu.roll(p2l, 3 * SSD_HEADS, 1))))
        spread = jnp.dot(packed.astype(BF16), emat_ref[...], preferred_element_type=F32)
        grow = spread[:, :sw]
        gend = spread[:, sw:]
        xs_c = xs[sl]
        xs_b = xs_c.astype(BF16)
        xd = (xs_c * gend).astype(BF16)
        ycols = []
        for g in range(SSD_GROUPS):
            bm_g = bm[sl, g * SSD_STATE:(g + 1) * SSD_STATE]
            cm_g = cm[sl, g * SSD_STATE:(g + 1) * SSD_STATE].astype(BF16)
            gc = slice(g * gcols, (g + 1) * gcols)
            cb = lax.dot_general(cm_g, bm_g.astype(BF16), (((1,), (1,)), ((), ())),
                                 preferred_element_type=F32)
            st = state_ref[g]
            y_off = jnp.dot(cm_g, st.astype(BF16), preferred_element_type=F32) * grow[:, gc]
            state_ref[g] = st * grow[SSD_L - 1:SSD_L, gc] + jnp.dot(
                bm_g.T.astype(BF16), xd[:, gc], preferred_element_type=F32)
            for p in range(hpg // 2):
                ws = []
                for hd in (g * hpg + 2 * p, g * hpg + 2 * p + 1):
                    seg = cs[:, hd:hd + 1] - cs_t[hd:hd + 1, :]
                    decay = jnp.where(causal, jnp.exp(jnp.where(causal, seg, 0.0)), 0.0)
                    ws.append((cb * decay * dt_t[hd:hd + 1, :]).astype(BF16))
                c0 = (g * hpg + 2 * p) * SSD_HEAD_DIM
                xp = xs_b[:, c0:c0 + LANES]
                zero = jnp.zeros_like(xp)
                rhs = jnp.concatenate([jnp.where(first_head, xp, zero),
                                       jnp.where(first_head, zero, xp)], axis=0)
                y_diag = jnp.dot(jnp.concatenate(ws, axis=1), rhs, preferred_element_type=F32)
                ycols.append(y_diag + y_off[:, 2 * p * SSD_HEAD_DIM:2 * p * SSD_HEAD_DIM + LANES])
        y_c = jnp.concatenate(ycols, axis=1) + dexp_ref[...] * xs_c
        yb_rows.append(_rms(y_c * _silu(z[sl]), sng_ref[...]))
    y_b = jnp.concatenate(yb_rows, axis=0).astype(BF16)

    mix = (jnp.dot(y_a, wout_ref[0:gw, :], preferred_element_type=F32)
           + jnp.dot(y_b, wout_ref[gw:gw + sw, :], preferred_element_type=F32))
    o_ref[0] = x + mod_ref[0, 2:3, :] * mix


def _hybrid(h, mod3, g, w_in, w_out, v_norm_g, w_s, b_s, conv_w, conv_b, dt_bias, a_log, d_skip,
            ssd_norm_g):
    b, s, d = h.shape
    gw = v_norm_g.shape[0]
    sw = ssd_norm_g.shape[0]
    cdim = conv_w.shape[1]
    tm = min(HYB_TM, s)
    o3 = 2 * gw + sw + cdim
    pad = LANES - SSD_HEADS
    w_in_b = w_in.astype(BF16)
    w_dt = jnp.pad(w_in_b[:, o3:], ((0, 0), (0, pad)))
    dtb = jnp.pad(dt_bias, (0, pad)).reshape(1, LANES)
    alog = jnp.pad(a_log, (0, pad)).reshape(1, LANES)
    dexp = jnp.repeat(d_skip, SSD_HEAD_DIM).reshape(1, sw)
    bs_full = jnp.repeat(b_s.T, gw // GM_GROUPS, axis=1)
    emat = np.zeros((LANES, 2 * sw), np.float32)
    for part in range(4):
        for hd in range(SSD_HEADS):
            c0 = (part // 2) * sw + hd * SSD_HEAD_DIM
            emat[part * SSD_HEADS + hd, c0:c0 + SSD_HEAD_DIM] = 1.0
    tril = np.tril(np.ones((SSD_L, SSD_L), np.float32))
    tok = pl.BlockSpec((1, tm, d), lambda bi, i: (bi, i, 0))
    operands = [
        g.reshape(1, d), w_in_b, w_dt,
        w_out.astype(BF16), v_norm_g.reshape(1, gw), w_s, bs_full, conv_w,
        conv_b.reshape(1, cdim), dtb, alog, dexp, ssd_norm_g.reshape(1, sw),
        jnp.asarray(emat, BF16), jnp.asarray(tril, BF16),
    ]
    return pl.pallas_call(
        _hyb_kernel,
        grid=(b, s // tm),
        in_specs=[tok, pl.BlockSpec((1, N_SUB, d), lambda bi, i: (bi, 0, 0))]
        + [_const_spec(op.shape) for op in operands],
        out_specs=tok,
        out_shape=jax.ShapeDtypeStruct(h.shape, F32),
        scratch_shapes=[
            pltpu.VMEM((tm + CONV_HALO, cdim), F32),
            pltpu.VMEM((SSD_GROUPS, SSD_STATE, sw // SSD_GROUPS), F32),
        ],
        compiler_params=_params("parallel", "arbitrary"),
        name="hybrid_mixer",
    )(h, mod3, *operands)


def _pair_rms(x, g, first_head):
    xx = x * x
    s0 = jnp.sum(jnp.where(first_head, xx, 0.0), axis=-1, keepdims=True)
    s1 = jnp.sum(jnp.where(first_head, 0.0, xx), axis=-1, keepdims=True)
    ms = jnp.where(first_head, s0, s1) * (1.0 / SB_HEAD_DIM)
    return x * lax.rsqrt(ms + RMS_EPS) * g


def _qkv_kernel(h_ref, mod_ref, g_ref, w_ref, qg_ref, kg_ref, q_ref, k_ref, v_ref):
    y = _norm_mod(h_ref[0], g_ref[...], mod_ref[0, 0:1, :], mod_ref[0, 1:2, :]).astype(BF16)
    width = q_ref.shape[2]
    first_head = lax.broadcasted_iota(jnp.int32, (1, LANES), 1) < SB_HEAD_DIM
    for part, (out, gain) in enumerate(((q_ref, qg_ref[...] * (SB_HEAD_DIM ** -0.5)),
                                        (k_ref, kg_ref[...]))):
        a = jnp.dot(y, w_ref[:, part * width:(part + 1) * width], preferred_element_type=F32)
        for c0 in range(0, width, LANES):
            out[0, :, c0:c0 + LANES] = _pair_rms(a[:, c0:c0 + LANES], gain, first_head).astype(BF16)
    v_ref[0] = jnp.dot(y, w_ref[:, 2 * width:3 * width], preferred_element_type=F32).astype(BF16)


def _qkv(h, mod3, g, w, q_norm_g, k_norm_g):
    b, s, d = h.shape
    width = w.shape[1] // 3
    tm = min(QKV_TM, s)
    out = pl.BlockSpec((1, tm, width), lambda bi, i: (bi, i, 0))
    return pl.pallas_call(
        _qkv_kernel,
        grid=(b, s // tm),
        in_specs=[
            pl.BlockSpec((1, tm, d), lambda bi, i: (bi, i, 0)),
            pl.BlockSpec((1, N_SUB, d), lambda bi, i: (bi, 0, 0)),
            _const_spec((1, d)),
            _const_spec((d, 3 * width)),
            _const_spec((1, LANES)),
            _const_spec((1, LANES)),
        ],
        out_specs=[out, out, out],
        out_shape=[jax.ShapeDtypeStruct((b, s, width), BF16)] * 3,
        compiler_params=_params("parallel", "arbitrary"),
        name="qkv_proj",
    )(h, mod3, g.reshape(1, d), w, jnp.tile(q_norm_g, 2).reshape(1, LANES),
      jnp.tile(k_norm_g, 2).reshape(1, LANES))


def _sb_tile(q_h, k_t, v_t, u, carry, mask):
    z = lax.dot_general(q_h, k_t, (((1,), (1,)), ((), ())), preferred_element_type=F32)
    log_beta = jnp.minimum(z, 0.0) - jnp.log(1.0 + jnp.exp(-jnp.abs(z)))
    log_keep = log_beta - z
    if mask is not None:
        log_keep = jnp.where(mask, log_keep, 0.0)
    hi = log_keep.astype(BF16)
    lo = (log_keep - hi.astype(F32)).astype(BF16)
    after = jnp.dot(jnp.concatenate([hi, lo], axis=1), u, preferred_element_type=F32)
    if carry is not None:
        after = after + carry
    wgt = jnp.exp(log_beta + after)
    if mask is not None:
        wgt = jnp.where(mask, wgt, 0.0)
    pv = jnp.dot(wgt.astype(BF16), v_t, preferred_element_type=F32)
    return pv, jnp.sum(log_keep, axis=-1, keepdims=True)


def _attn_kernel(q_ref, k_ref, v_ref, u_ref, o_ref, acc_ref, carry_ref):
    t = q_ref.shape[1]
    npair = q_ref.shape[2] // LANES
    i = pl.program_id(2)
    first_head = lax.broadcasted_iota(jnp.int32, (1, LANES), 1) < SB_HEAD_DIM
    u = u_ref[...]
    diag = (lax.broadcasted_iota(jnp.int32, (t, t), 1) < lax.broadcasted_iota(jnp.int32, (t, t), 0))
    diag2 = jnp.concatenate([diag, diag], axis=0)
    lanes = [slice(p * LANES, (p + 1) * LANES) for p in range(npair)]
    q2 = []
    for p in range(npair):
        q = q_ref[0, :, lanes[p]]
        zero = jnp.zeros_like(q)
        q2.append(jnp.concatenate([jnp.where(first_head, q, zero),
                                   jnp.where(first_head, zero, q)], axis=0))

    def kv_tile(j, p):
        k0 = pl.multiple_of(j * t, t)
        return k_ref[0, pl.ds(k0, t), lanes[p]], v_ref[0, pl.ds(k0, t), lanes[p]]

    def store_out():
        for p in range(npair):
            o_ref[0, :, lanes[p]] = jnp.where(first_head, acc_ref[p, 0:t, :],
                                              acc_ref[p, t:2 * t, :]).astype(o_ref.dtype)

    @pl.when(i == 0)
    def _():
        for p in range(npair):
            k_d, v_d = kv_tile(i, p)
            acc_ref[p] = _sb_tile(q2[p], k_d, v_d, u, None, diag2)[0]
        store_out()

    @pl.when(i > 0)
    def _():
        worst = None
        for p in range(npair):
            k_d, v_d = kv_tile(i, p)
            k_p, v_p = kv_tile(i - 1, p)
            pv_d, c_d = _sb_tile(q2[p], k_d, v_d, u, None, diag2)
            pv_p, c_p = _sb_tile(q2[p], k_p, v_p, u, c_d, None)
            acc_ref[p] = pv_d + pv_p
            c = c_d + c_p
            carry_ref[p] = jnp.broadcast_to(c, (2 * t, LANES))
            worst = c if worst is None else jnp.maximum(worst, c)

        def cond(state):
            j, top = state
            return jnp.logical_and(j >= 0, top > SB_EXP_FLOOR)

        def body(state):
            j, _ = state
            worst = None
            for p in range(npair):
                k_j, v_j = kv_tile(j, p)
                c = carry_ref[p, :, 0:1]
                pv, rs = _sb_tile(q2[p], k_j, v_j, u, c, None)
                acc_ref[p] += pv
                c = c + rs
                carry_ref[p] = jnp.broadcast_to(c, (2 * t, LANES))
                worst = c if worst is None else jnp.maximum(worst, c)
            return j - 1, jnp.max(worst)

        lax.while_loop(cond, body, (i - 2, jnp.max(worst)))
        store_out()


def _attention(q, k, v):
    b, s, width = q.shape
    t = min(ATT_T, s)
    gw = ATT_PAIRS * LANES
    umat = np.tril(np.ones((t, t), np.float32), -1)
    umat = np.concatenate([umat, umat], axis=0)
    q_spec = pl.BlockSpec((1, t, gw), lambda bi, p, i: (bi, i, p))
    kv_spec = pl.BlockSpec((1, s, gw), lambda bi, p, i: (bi, 0, p))
    return pl.pallas_call(
        _attn_kernel,
        grid=(b, width // gw, s // t),
        in_specs=[q_spec, kv_spec, kv_spec, _const_spec((2 * t, t))],
        out_specs=q_spec,
        out_shape=jax.ShapeDtypeStruct((b, s, width), BF16),
        scratch_shapes=[
            pltpu.VMEM((ATT_PAIRS, 2 * t, LANES), F32),
            pltpu.VMEM((ATT_PAIRS, 2 * t, LANES), F32),
        ],
        compiler_params=_params("parallel", "parallel", "arbitrary"),
        name="stickbreaking_attention",
    )(q, k, v, jnp.asarray(umat, BF16))


def kernel(x, c, mod_w, mod_b, norm_g, ffn_w_gate, ffn_w_up, ffn_w_down, hy_w_in, hy_w_out, gm_v_norm_g, gm_w_s, gm_b_s, ssd_conv_w, ssd_conv_b, ssd_dt_bias, ssd_a_log, ssd_d, ssd_norm_g, sb_w_qkv, sb_q_norm_g, sb_k_norm_g, sb_w_o):
    depth = mod_w.shape[0]
    b, s, d = x.shape
    mod = _modulation(c, mod_w, mod_b).reshape(depth, b, N_SUB, 3, d)
    wg = ffn_w_gate.astype(BF16)
    wu = ffn_w_up.astype(BF16)
    wd = ffn_w_down.astype(BF16)
    h = x
    for layer in range(depth):
        j = layer // 2
        h = _ffn(h, mod[layer, :, 0], norm_g[layer, 0], wg, wu, wd, (layer, 0), None)
        pre = None
        if layer % 2 == 0:
            h = _hybrid(h, mod[layer, :, 1], norm_g[layer, 1], hy_w_in[j], hy_w_out[j],
                        gm_v_norm_g[j], gm_w_s[j], gm_b_s[j], ssd_conv_w[j], ssd_conv_b[j],
                        ssd_dt_bias[j], ssd_a_log[j], ssd_d[j], ssd_norm_g[j])
        else:
            q, k, v = _qkv(h, mod[layer, :, 1], norm_g[layer, 1], sb_w_qkv[j].astype(BF16),
                           sb_q_norm_g[j], sb_k_norm_g[j])
            pre = (_attention(q, k, v), mod[layer, :, 1], sb_w_o[j].astype(BF16))
        h = _ffn(h, mod[layer, :, 2], norm_g[layer, 2], wg, wu, wd, (layer, 1), norm_g[layer, 3],
                 pre)
    return h
```

```python
import functools
import math

import numpy as np
import jax
import jax.numpy as jnp
from jax import lax
from jax.experimental import pallas as pl
from jax.experimental.pallas import tpu as pltpu

F32 = jnp.float32
BF16 = jnp.bfloat16
HIGHEST = lax.Precision.HIGHEST

LANES = 128
VMEM_LIMIT = 56 * 1024 * 1024

RMS_EPS = 1e-6
N_SUB = 3
GM_BLOCK = 128
GM_GROUPS = 8
CHUNK = 64
SSD_HEAD_DIM = 64
SSD_HEADS = 16
SSD_GROUPS = 2
SSD_STATE = 128
SSD_CONV = 4
SSD_L = 128
SB_HEAD_DIM = 64
SB_EXP_FLOOR = -105.0
CONV_HALO = 8

MOD_TN = 2304
FFN_TM = 1024
FFN_FC = 256
FFN_ROW_BLOCKS = 4
HYB_TM = 512
QKV_TM = 1024
ATT_T = 256
ATT_PAIRS = 8


def _silu(x):
    return x / (1.0 + jnp.exp(-x))


def _softplus(x):
    return jnp.maximum(x, 0.0) + jnp.log1p(jnp.exp(-jnp.abs(x)))


def _gelu_tanh(x):
    c = math.sqrt(2.0 / math.pi)
    return x * (0.5 * (1.0 + jnp.tanh(c * (x + 0.044715 * (x * x * x)))))


def _rms(x, g):
    ms = jnp.mean(x * x, axis=-1, keepdims=True)
    return x * lax.rsqrt(ms + RMS_EPS) * g


def _bf16_split(x):
    hi = x.astype(BF16).astype(F32)
    return hi, x - hi


def _norm_mod(x, g, shift, scale):
    ms = jnp.mean(x * x, axis=-1, keepdims=True)
    return x * lax.rsqrt(ms + RMS_EPS) * (g * (1.0 + scale)) + shift


def _const_spec(shape):
    nd = len(shape)
    return pl.BlockSpec(shape, lambda *_: (0,) * nd, pipeline_mode=pl.Buffered(1))


def _params(*sem):
    return pltpu.CompilerParams(dimension_semantics=sem, vmem_limit_bytes=VMEM_LIMIT)


def _mod_kernel(c_ref, w_ref, b_ref, o_ref):
    cond = _silu(c_ref[...])
    o_ref[0] = jnp.dot(cond, w_ref[0], precision=HIGHEST, preferred_element_type=F32) + b_ref[0]


def _modulation(c, mod_w, mod_b):
    depth, d, width = mod_w.shape
    b = c.shape[0]
    tn = MOD_TN
    return pl.pallas_call(
        _mod_kernel,
        grid=(depth, width // tn),
        in_specs=[
            pl.BlockSpec((b, d), lambda l, j: (0, 0)),
            pl.BlockSpec((1, d, tn), lambda l, j: (l, 0, j)),
            pl.BlockSpec((1, 1, tn), lambda l, j: (l, 0, j)),
        ],
        out_specs=pl.BlockSpec((1, b, tn), lambda l, j: (l, 0, j)),
        out_shape=jax.ShapeDtypeStruct((depth, b, width), F32),
        compiler_params=_params("arbitrary", "arbitrary"),
        name="modulation",
    )(c, mod_w, mod_b.reshape(depth, 1, width))


def _ffn_kernel(h_ref, mod_ref, g_ref, wg_ref, wu_ref, wd_ref, fg_ref, *rest, final_norm,
                pre_proj):
    o_ref = rest[-1]
    x = h_ref[0]
    if pre_proj:
        a_ref, pmod_ref, wo_ref = rest[:3]
        x = x + pmod_ref[0, 2:3, :] * jnp.dot(a_ref[0], wo_ref[...], preferred_element_type=F32)
    rb = x.shape[0] // FFN_ROW_BLOCKS
    y_blocks = [
        _norm_mod(x[r0:r0 + rb], g_ref[...], mod_ref[0, 0:1, :], mod_ref[0, 1:2, :]).astype(BF16)
        for r0 in range(0, x.shape[0], rb)]
    y = jnp.concatenate(y_blocks, axis=0)
    acc = jnp.zeros(x.shape, F32)
    for f0 in range(0, wg_ref.shape[1], FFN_FC):
        lhs = y_blocks if f0 == 0 else [y]
        a = jnp.concatenate([jnp.dot(yb, wg_ref[:, f0:f0 + FFN_FC], preferred_element_type=F32)
                             for yb in lhs], axis=0)
        b = jnp.concatenate([jnp.dot(yb, wu_ref[:, f0:f0 + FFN_FC], preferred_element_type=F32)
                             for yb in lhs], axis=0)
        hid = (_silu(a) * b).astype(BF16)
        acc = acc + jnp.dot(hid, wd_ref[f0:f0 + FFN_FC, :], preferred_element_type=F32)
    out = x + 0.5 * mod_ref[0, 2:3, :] * acc
    if final_norm:
        out = _rms(out, fg_ref[...])
    o_ref[0] = out


def _ffn(h, mod3, g, wg, wu, wd, widx, final_g, pre=None):
    b, s, d = h.shape
    f = wg.shape[3]
    tm = min(FFN_TM, s)
    final_norm = final_g is not None
    fg = final_g if final_norm else g
    tok = pl.BlockSpec((1, tm, d), lambda bi, i: (bi, i, 0))
    mod_spec = pl.BlockSpec((1, N_SUB, d), lambda bi, i: (bi, 0, 0))

    def w_spec(rows, cols):
        return pl.BlockSpec((None, None, rows, cols), lambda bi, i: (*widx, 0, 0),
                            pipeline_mode=pl.Buffered(1))

    in_specs = [tok, mod_spec, _const_spec((1, d)), w_spec(d, f), w_spec(d, f), w_spec(f, d),
                _const_spec((1, d))]
    operands = [h, mod3, g.reshape(1, d), wg, wu, wd, fg.reshape(1, d)]
    if pre is not None:
        a, pmod3, w_o = pre
        in_specs += [pl.BlockSpec((1, tm, a.shape[2]), lambda bi, i: (bi, i, 0)), mod_spec,
                     _const_spec(w_o.shape)]
        operands += [a, pmod3, w_o]
    return pl.pallas_call(
        functools.partial(_ffn_kernel, final_norm=final_norm, pre_proj=pre is not None),
        grid=(b, s // tm),
        in_specs=in_specs,
        out_specs=tok,
        out_shape=jax.ShapeDtypeStruct(h.shape, F32),
        compiler_params=_params("parallel", "arbitrary"),
        name="ffn_proj" if pre is not None else "ffn",
    )(*operands)


def _hyb_kernel(h_ref, mod_ref, g_ref, win_ref, wdt_ref, wout_ref,
                vng_ref, ws_ref, bs_ref, convw_ref, convb_ref, dtb_ref, alog_ref, dexp_ref,
                sng_ref, emat_ref, tril_ref, o_ref, conv_buf, state_ref):
    tm = h_ref.shape[1]
    gw = vng_ref.shape[1]
    sw = sng_ref.shape[1]
    o1 = 2 * gw
    o2 = o1 + sw
    o3 = o2 + conv_buf.shape[1]
    gs = SSD_GROUPS * SSD_STATE
    hpg = SSD_HEADS // SSD_GROUPS
    gcols = hpg * SSD_HEAD_DIM

    @pl.when(pl.program_id(1) == 0)
    def _():
        conv_buf[0:CONV_HALO, :] = jnp.zeros((CONV_HALO, conv_buf.shape[1]), F32)
        state_ref[...] = jnp.zeros(state_ref.shape, F32)

    x = h_ref[0]
    y = _norm_mod(x, g_ref[...], mod_ref[0, 0:1, :], mod_ref[0, 1:2, :]).astype(BF16)

    gm = _gelu_tanh(jnp.dot(y, win_ref[:, 0:o1], preferred_element_type=F32))
    u = gm[:, :gw]
    v = gm[:, gw:]
    pos_t = lax.broadcasted_iota(jnp.int32, (GM_BLOCK, GM_BLOCK), 0)
    pos_s = lax.broadcasted_iota(jnp.int32, (GM_BLOCK, GM_BLOCK), 1)
    blk_causal = (pos_s // CHUNK) <= (pos_t // CHUNK)
    ya_cols = []
    for g in range(GM_GROUPS):
        c0 = g * LANES
        vn = _rms(v[:, c0:c0 + LANES], vng_ref[:, c0:c0 + LANES]).astype(BF16)
        wm = jnp.where(blk_causal, ws_ref[g], 0.0).astype(BF16)
        rows = []
        for r0 in range(0, tm, GM_BLOCK):
            mixed = jnp.dot(wm, vn[r0:r0 + GM_BLOCK, :], preferred_element_type=F32)
            rows.append(mixed + bs_ref[:, c0:c0 + LANES])
        ya_cols.append(u[:, c0:c0 + LANES] * jnp.concatenate(rows, axis=0))
    y_a = jnp.concatenate(ya_cols, axis=1).astype(BF16)

    z = jnp.dot(y, win_ref[:, o1:o2], preferred_element_type=F32)
    conv_buf[CONV_HALO:CONV_HALO + tm, :] = jnp.dot(y, win_ref[:, o2:o3], preferred_element_type=F32)
    dt = _softplus(jnp.dot(y, wdt_ref[...], preferred_element_type=F32) + dtb_ref[...])
    xr = conv_buf[CONV_HALO:CONV_HALO + tm, :]
    halo = conv_buf[0:CONV_HALO, :]
    row8 = lax.broadcasted_iota(jnp.int32, halo.shape, 0)
    acc = convb_ref[...] + xr * convw_ref[SSD_CONV - 1:SSD_CONV, :]
    for j in range(1, SSD_CONV):
        rolled = pltpu.roll(xr, j, 0)
        first = jnp.where(row8 < j, pltpu.roll(halo, j, 0), rolled[0:CONV_HALO])
        shifted = jnp.concatenate([first, rolled[CONV_HALO:]], axis=0)
        acc = acc + shifted * convw_ref[SSD_CONV - 1 - j:SSD_CONV - j, :]
    conv_buf[0:CONV_HALO, :] = conv_buf[tm:tm + CONV_HALO, :]
    xbc = _silu(acc)
    xs = xbc[:, :sw]
    bm = xbc[:, sw:sw + gs]
    cm = xbc[:, sw + gs:sw + 2 * gs]
    dta = dt * (-jnp.exp(alog_ref[...]))

    row_i = lax.broadcasted_iota(jnp.int32, (SSD_L, SSD_L), 0)
    col_i = lax.broadcasted_iota(jnp.int32, (SSD_L, SSD_L), 1)
    causal = row_i >= col_i
    lane = lax.broadcasted_iota(jnp.int32, (SSD_L, LANES), 1)
    first_head = lane < SSD_HEAD_DIM
    yb_rows = []
    for r0 in range(0, tm, SSD_L):
        sl = slice(r0, r0 + SSD_L)
        dt_c = dt[sl]
        t0, rest = _bf16_split(dta[sl])
        t1, rest = _bf16_split(rest)
        terms = jnp.concatenate([t0, t1, rest], axis=1).astype(BF16)
        sums = jnp.dot(tril_ref[...], terms, preferred_element_type=F32)
        cs = sums[:, :LANES] + sums[:, LANES:2 * LANES] + sums[:, 2 * LANES:]
        cs_end = cs[SSD_L - 1:SSD_L, :]
        cs_t = cs.T
        dt_t = dt_c.T
        p1h, p1l = _bf16_split(jnp.exp(cs))
        p2h, p2l = _bf16_split(jnp.exp(cs_end - cs) * dt_c)
        packed = jnp.where(
            lane < SSD_HEADS, p1h,
            jnp.where(lane < 2 * SSD_HEADS, pltpu.roll(p1l, SSD_HEADS, 1),
                      jnp.where(lane < 3 * SSD_HEADS, pltpu.roll(p2h, 2 * SSD_HEADS, 1),
                                pltpu.roll(p2l, 3 * SSD_HEADS, 1))))
        spread = jnp.dot(packed.astype(BF16), emat_ref[...], preferred_element_type=F32)
        grow = spread[:, :sw]
        gend = spread[:, sw:]
        xs_c = xs[sl]
        xs_b = xs_c.astype(BF16)
        xd = (xs_c * gend).astype(BF16)
        ycols = []
        for g in range(SSD_GROUPS):
            bm_g = bm[sl, g * SSD_STATE:(g + 1) * SSD_STATE]
            cm_g = cm[sl, g * SSD_STATE:(g + 1) * SSD_STATE].astype(BF16)
            gc = slice(g * gcols, (g + 1) * gcols)
            cb = lax.dot_general(cm_g, bm_g.astype(BF16), (((1,), (1,)), ((), ())),
                                 preferred_element_type=F32)
            st = state_ref[g]
            y_off = jnp.dot(cm_g, st.astype(BF16), preferred_element_type=F32) * grow[:, gc]
            state_ref[g] = st * grow[SSD_L - 1:SSD_L, gc] + jnp.dot(
                bm_g.T.astype(BF16), xd[:, gc], preferred_element_type=F32)
            for p in range(hpg // 2):
                ws = []
                for hd in (g * hpg + 2 * p, g * hpg + 2 * p + 1):
                    seg = cs[:, hd:hd + 1] - cs_t[hd:hd + 1, :]
                    decay = jnp.where(causal, jnp.exp(jnp.where(causal, seg, 0.0)), 0.0)
                    ws.append((cb * decay * dt_t[hd:hd + 1, :]).astype(BF16))
                c0 = (g * hpg + 2 * p) * SSD_HEAD_DIM
                xp = xs_b[:, c0:c0 + LANES]
                zero = jnp.zeros_like(xp)
                rhs = jnp.concatenate([jnp.where(first_head, xp, zero),
                                       jnp.where(first_head, zero, xp)], axis=0)
                y_diag = jnp.dot(jnp.concatenate(ws, axis=1), rhs, preferred_element_type=F32)
                ycols.append(y_diag + y_off[:, 2 * p * SSD_HEAD_DIM:2 * p * SSD_HEAD_DIM + LANES])
        y_c = jnp.concatenate(ycols, axis=1) + dexp_ref[...] * xs_c
        yb_rows.append(_rms(y_c * _silu(z[sl]), sng_ref[...]))
    y_b = jnp.concatenate(yb_rows, axis=0).astype(BF16)

    mix = (jnp.dot(y_a, wout_ref[0:gw, :], preferred_element_type=F32)
           + jnp.dot(y_b, wout_ref[gw:gw + sw, :], preferred_element_type=F32))
    o_ref[0] = x + mod_ref[0, 2:3, :] * mix


def _hybrid(h, mod3, g, w_in, w_out, v_norm_g, w_s, b_s, conv_w, conv_b, dt_bias, a_log, d_skip,
            ssd_norm_g):
    b, s, d = h.shape
    gw = v_norm_g.shape[0]
    sw = ssd_norm_g.shape[0]
    cdim = conv_w.shape[1]
    tm = min(HYB_TM, s)
    o3 = 2 * gw + sw + cdim
    pad = LANES - SSD_HEADS
    w_in_b = w_in.astype(BF16)
    w_dt = jnp.pad(w_in_b[:, o3:], ((0, 0), (0, pad)))
    dtb = jnp.pad(dt_bias, (0, pad)).reshape(1, LANES)
    alog = jnp.pad(a_log, (0, pad)).reshape(1, LANES)
    dexp = jnp.repeat(d_skip, SSD_HEAD_DIM).reshape(1, sw)
    bs_full = jnp.repeat(b_s.T, gw // GM_GROUPS, axis=1)
    emat = np.zeros((LANES, 2 * sw), np.float32)
    for part in range(4):
        for hd in range(SSD_HEADS):
            c0 = (part // 2) * sw + hd * SSD_HEAD_DIM
            emat[part * SSD_HEADS + hd, c0:c0 + SSD_HEAD_DIM] = 1.0
    tril = np.tril(np.ones((SSD_L, SSD_L), np.float32))
    tok = pl.BlockSpec((1, tm, d), lambda bi, i: (bi, i, 0))
    operands = [
        g.reshape(1, d), w_in_b, w_dt,
        w_out.astype(BF16), v_norm_g.reshape(1, gw), w_s, bs_full, conv_w,
        conv_b.reshape(1, cdim), dtb, alog, dexp, ssd_norm_g.reshape(1, sw),
        jnp.asarray(emat, BF16), jnp.asarray(tril, BF16),
    ]
    return pl.pallas_call(
        _hyb_kernel,
        grid=(b, s // tm),
        in_specs=[tok, pl.BlockSpec((1, N_SUB, d), lambda bi, i: (bi, 0, 0))]
        + [_const_spec(op.shape) for op in operands],
        out_specs=tok,
        out_shape=jax.ShapeDtypeStruct(h.shape, F32),
        scratch_shapes=[
            pltpu.VMEM((tm + CONV_HALO, cdim), F32),
            pltpu.VMEM((SSD_GROUPS, SSD_STATE, sw // SSD_GROUPS), F32),
        ],
        compiler_params=_params("parallel", "arbitrary"),
        name="hybrid_mixer",
    )(h, mod3, *operands)


def _pair_rms(x, g, first_head):
    xx = x * x
    s0 = jnp.sum(jnp.where(first_head, xx, 0.0), axis=-1, keepdims=True)
    s1 = jnp.sum(jnp.where(first_head, 0.0, xx), axis=-1, keepdims=True)
    ms = jnp.where(first_head, s0, s1) * (1.0 / SB_HEAD_DIM)
    return x * lax.rsqrt(ms + RMS_EPS) * g


def _qkv_kernel(h_ref, mod_ref, g_ref, w_ref, qg_ref, kg_ref, q_ref, k_ref, v_ref):
    y = _norm_mod(h_ref[0], g_ref[...], mod_ref[0, 0:1, :], mod_ref[0, 1:2, :]).astype(BF16)
    width = q_ref.shape[2]
    first_head = lax.broadcasted_iota(jnp.int32, (1, LANES), 1) < SB_HEAD_DIM
    for part, (out, gain) in enumerate(((q_ref, qg_ref[...] * (SB_HEAD_DIM ** -0.5)),
                                        (k_ref, kg_ref[...]))):
        a = jnp.dot(y, w_ref[:, part * width:(part + 1) * width], preferred_element_type=F32)
        for c0 in range(0, width, LANES):
            out[0, :, c0:c0 + LANES] = _pair_rms(a[:, c0:c0 + LANES], gain, first_head).astype(BF16)
    v_ref[0] = jnp.dot(y, w_ref[:, 2 * width:3 * width], preferred_element_type=F32).astype(BF16)


def _qkv(h, mod3, g, w, q_norm_g, k_norm_g):
    b, s, d = h.shape
    width = w.shape[1] // 3
    tm = min(QKV_TM, s)
    out = pl.BlockSpec((1, tm, width), lambda bi, i: (bi, i, 0))
    return pl.pallas_call(
        _qkv_kernel,
        grid=(b, s // tm),
        in_specs=[
            pl.BlockSpec((1, tm, d), lambda bi, i: (bi, i, 0)),
            pl.BlockSpec((1, N_SUB, d), lambda bi, i: (bi, 0, 0)),
            _const_spec((1, d)),
            _const_spec((d, 3 * width)),
            _const_spec((1, LANES)),
            _const_spec((1, LANES)),
        ],
        out_specs=[out, out, out],
        out_shape=[jax.ShapeDtypeStruct((b, s, width), BF16)] * 3,
        compiler_params=_params("parallel", "arbitrary"),
        name="qkv_proj",
    )(h, mod3, g.reshape(1, d), w, jnp.tile(q_norm_g, 2).reshape(1, LANES),
      jnp.tile(k_norm_g, 2).reshape(1, LANES))


def _sb_tile(q_h, k_t, v_t, u, carry, mask):
    z = lax.dot_general(q_h, k_t, (((1,), (1,)), ((), ())), preferred_element_type=F32)
    log_beta = jnp.minimum(z, 0.0) - jnp.log(1.0 + jnp.exp(-jnp.abs(z)))
    log_keep = log_beta - z
    if mask is not None:
        log_keep = jnp.where(mask, log_keep, 0.0)
    hi = log_keep.astype(BF16)
    lo = (log_keep - hi.astype(F32)).astype(BF16)
    after = jnp.dot(jnp.concatenate([hi, lo], axis=1), u, preferred_element_type=F32)
    if carry is not None:
        after = after + carry
    wgt = jnp.exp(log_beta + after)
    if mask is not None:
        wgt = jnp.where(mask, wgt, 0.0)
    pv = jnp.dot(wgt.astype(BF16), v_t, preferred_element_type=F32)
    return pv, jnp.sum(log_keep, axis=-1, keepdims=True)


def _attn_kernel(q_ref, k_ref, v_ref, u_ref, o_ref, acc_ref, carry_ref):
    t = q_ref.shape[1]
    npair = q_ref.shape[2] // LANES
    i = pl.program_id(2)
    first_head = lax.broadcasted_iota(jnp.int32, (1, LANES), 1) < SB_HEAD_DIM
    u = u_ref[...]
    diag = (lax.broadcasted_iota(jnp.int32, (t, t), 1) < lax.broadcasted_iota(jnp.int32, (t, t), 0))
    diag2 = jnp.concatenate([diag, diag], axis=0)
    lanes = [slice(p * LANES, (p + 1) * LANES) for p in range(npair)]
    q2 = []
    for p in range(npair):
        q = q_ref[0, :, lanes[p]]
        zero = jnp.zeros_like(q)
        q2.append(jnp.concatenate([jnp.where(first_head, q, zero),
                                   jnp.where(first_head, zero, q)], axis=0))

    def kv_tile(j, p):
        k0 = pl.multiple_of(j * t, t)
        return k_ref[0, pl.ds(k0, t), lanes[p]], v_ref[0, pl.ds(k0, t), lanes[p]]

    def store_out():
        for p in range(npair):
            o_ref[0, :, lanes[p]] = jnp.where(first_head, acc_ref[p, 0:t, :],
                                              acc_ref[p, t:2 * t, :]).astype(o_ref.dtype)

    @pl.when(i == 0)
    def _():
        for p in range(npair):
            k_d, v_d = kv_tile(i, p)
            acc_ref[p] = _sb_tile(q2[p], k_d, v_d, u, None, diag2)[0]
        store_out()

    @pl.when(i > 0)
    def _():
        worst = None
        for p in range(npair):
            k_d, v_d = kv_tile(i, p)
            k_p, v_p = kv_tile(i - 1, p)
            pv_d, c_d = _sb_tile(q2[p], k_d, v_d, u, None, diag2)
            pv_p, c_p = _sb_tile(q2[p], k_p, v_p, u, c_d, None)
            acc_ref[p] = pv_d + pv_p
            c = c_d + c_p
            carry_ref[p] = jnp.broadcast_to(c, (2 * t, LANES))
            worst = c if worst is None else jnp.maximum(worst, c)

        def cond(state):
            j, top = state
            return jnp.logical_and(j >= 0, top > SB_EXP_FLOOR)

        def body(state):
            j, _ = state
            worst = None
            for p in range(npair):
                k_j, v_j = kv_tile(j, p)
                c = carry_ref[p, :, 0:1]
                pv, rs = _sb_tile(q2[p], k_j, v_j, u, c, None)
                acc_ref[p] += pv
                c = c + rs
                carry_ref[p] = jnp.broadcast_to(c, (2 * t, LANES))
                worst = c if worst is None else jnp.maximum(worst, c)
            return j - 1, jnp.max(worst)

        lax.while_loop(cond, body, (i - 2, jnp.max(worst)))
        store_out()


def _attention(q, k, v):
    b, s, width = q.shape
    t = min(ATT_T, s)
    gw = ATT_PAIRS * LANES
    umat = np.tril(np.ones((t, t), np.float32), -1)
    umat = np.concatenate([umat, umat], axis=0)
    q_spec = pl.BlockSpec((1, t, gw), lambda bi, p, i: (bi, i, p))
    kv_spec = pl.BlockSpec((1, s, gw), lambda bi, p, i: (bi, 0, p))
    return pl.pallas_call(
        _attn_kernel,
        grid=(b, width // gw, s // t),
        in_specs=[q_spec, kv_spec, kv_spec, _const_spec((2 * t, t))],
        out_specs=q_spec,
        out_shape=jax.ShapeDtypeStruct((b, s, width), BF16),
        scratch_shapes=[
            pltpu.VMEM((ATT_PAIRS, 2 * t, LANES), F32),
            pltpu.VMEM((ATT_PAIRS, 2 * t, LANES), F32),
        ],
        compiler_params=_params("parallel", "parallel", "arbitrary"),
        name="stickbreaking_attention",
    )(q, k, v, jnp.asarray(umat, BF16))


def kernel(x, c, mod_w, mod_b, norm_g, ffn_w_gate, ffn_w_up, ffn_w_down, hy_w_in, hy_w_out, gm_v_norm_g, gm_w_s, gm_b_s, ssd_conv_w, ssd_conv_b, ssd_dt_bias, ssd_a_log, ssd_d, ssd_norm_g, sb_w_qkv, sb_q_norm_g, sb_k_norm_g, sb_w_o):
    depth = mod_w.shape[0]
    b, s, d = x.shape
    mod = _modulation(c, mod_w, mod_b).reshape(depth, b, N_SUB, 3, d)
    wg = ffn_w_gate.astype(BF16)
    wu = ffn_w_up.astype(BF16)
    wd = ffn_w_down.astype(BF16)
    h = x
    for layer in range(depth):
        j = layer // 2
        h = _ffn(h, mod[layer, :, 0], norm_g[layer, 0], wg, wu, wd, (layer, 0), None)
        pre = None
        if layer % 2 == 0:
            h = _hybrid(h, mod[layer, :, 1], norm_g[layer, 1], hy_w_in[j], hy_w_out[j],
                        gm_v_norm_g[j], gm_w_s[j], gm_b_s[j], ssd_conv_w[j], ssd_conv_b[j],
                        ssd_dt_bias[j], ssd_a_log[j], ssd_d[j], ssd_norm_g[j])
        else:
            q, k, v = _qkv(h, mod[layer, :, 1], norm_g[layer, 1], sb_w_qkv[j].astype(BF16),
                           sb_q_norm_g[j], sb_k_norm_g[j])
            pre = (_attention(q, k, v), mod[layer, :, 1], sb_w_o[j].astype(BF16))
        h = _ffn(h, mod[layer, :, 2], norm_g[layer, 2], wg, wu, wd, (layer, 1), norm_g[layer, 3],
                 pre)
    return h
```
